```python
import math
import jax, jax.numpy as jnp
from jax import lax
import numpy as np

D_MODEL = 1024
BATCH = 2
SEQ = 8192
DEPTH = 4
DEC_BATCH = 128
DEC_SEQ = 1
PAST_LEN = 8192
PAGE_SIZE = 128

BRANCH_W = D_MODEL // 2
DIFF_HD = 64
DIFF_H = BRANCH_W // (2 * DIFF_HD)
DIFF_KV_W = 4 * DIFF_HD
MLA_NOPE = 64
MLA_ROPE = 32
MLA_V = 64
MLA_H = BRANCH_W // MLA_V
MLA_Q_LORA = D_MODEL // 4
MLA_KV_LORA = D_MODEL // 8
ROPE_THETA = 10000.0
NSA_HD = 64
NSA_H = BRANCH_W // NSA_HD
CMP_BLOCK = 32
CMP_STRIDE = 16
CMP_HIDDEN = 128
SEL_BLOCK = 64
N_SEL = 16
WINDOW = 512
N_MEM = 256
MEM_H = 4
MEM_HD = 64
D_FF = -(-(8 * D_MODEL) // (3 * 256)) * 256
Q_BLOCK = 128
N_BRANCH = 3
IN_SPLITS = (DIFF_H * 2 * DIFF_HD, DIFF_KV_W, MLA_Q_LORA, MLA_KV_LORA, MLA_ROPE,
             NSA_H * NSA_HD, 4 * NSA_HD, 2 * NSA_HD, NSA_H * 3, N_BRANCH * D_MODEL)
N_IN = sum(IN_SPLITS)
NEG = -1e30
FORCE = 1e6
EPS = 1e-6

kernel_name = 'hybrid_diff_mla_nsa_decoder_step'


def rmsnorm(x, g):
    xf = x.astype(jnp.float32)
    y = xf * lax.rsqrt(jnp.mean(xf * xf, axis=-1, keepdims=True) + EPS)
    return (y * g.astype(jnp.float32)).astype(x.dtype)


def alibi_slopes(n):
    return jnp.asarray(2.0 ** (-8.0 * (np.arange(n) + 1) / n), dtype=jnp.float32)


def rope(x, pos):
    half = x.shape[-1] // 2
    inv = ROPE_THETA ** (-np.arange(half, dtype=np.float32) / half)
    ang = pos.astype(jnp.float32)[:, None] * jnp.asarray(inv, jnp.float32)
    ang = ang.reshape(ang.shape[0], *([1] * (x.ndim - 3)), half)
    cos, sin = jnp.cos(ang), jnp.sin(ang)
    xf = x.astype(jnp.float32)
    x1, x2 = xf[..., :half], xf[..., half:]
    return jnp.concatenate([x1 * cos - x2 * sin, x1 * sin + x2 * cos], axis=-1).astype(x.dtype)


def map_query_blocks(fn, qpos, *qs):
    sq = qpos.shape[0]
    if sq <= Q_BLOCK:
        return fn(qpos, *qs)
    nb = -(-sq // Q_BLOCK)
    pad = nb * Q_BLOCK - sq
    qpos_b = jnp.pad(qpos, (0, pad), mode='edge').reshape(nb, Q_BLOCK)

    def split(a):
        a = jnp.pad(a, [(0, 0), (0, pad)] + [(0, 0)] * (a.ndim - 2))
        return jnp.moveaxis(a.reshape(a.shape[0], nb, Q_BLOCK, *a.shape[2:]), 1, 0)

    out = lax.map(lambda args: fn(*args), (qpos_b,) + tuple(split(a) for a in qs))
    out = jnp.moveaxis(out, 0, 1)
    return out.reshape(out.shape[0], nb * Q_BLOCK, *out.shape[3:])[:, :sq]


def diff_attention(q, k, v, qpos, kpos, lam):
    slopes = alibi_slopes(DIFF_H)
    scale = DIFF_HD ** -0.5

    def block(qp, qb):
        s = jnp.einsum('bqhcd,bkcd->bhcqk', qb, k).astype(jnp.float32) * scale
        dist = qp[:, None] - kpos[None, :]
        s = jnp.where(dist >= 0, s - slopes[:, None, None, None] * dist, NEG)
        p = jax.nn.softmax(s, axis=-1)
        a = p[:, :, 0] - lam * p[:, :, 1]
        return jnp.einsum('bhqk,bke->bqhe', a.astype(v.dtype), v)

    return map_query_blocks(block, qpos, q)


def mla_attention(q_lat, q_rope, ckv, krope, qpos, kpos):
    scale = (MLA_NOPE + MLA_ROPE) ** -0.5

    def block(qp, ql, qr):
        s = (jnp.einsum('bqhc,bkc->bhqk', ql, ckv)
             + jnp.einsum('bqhr,bkr->bhqk', qr, krope)).astype(jnp.float32) * scale
        s = jnp.where(qp[:, None] >= kpos[None, :], s, NEG)
        p = jax.nn.softmax(s, axis=-1).astype(ckv.dtype)
        return jnp.einsum('bhqk,bkc->bqhc', p, ckv)

    return map_query_blocks(block, qpos, q_lat, q_rope)


def compress_blocks(x, pe, w1, b1, w2):
    B, lp, D = x.shape
    nh = lp // CMP_STRIDE
    r = CMP_BLOCK // CMP_STRIDE
    nc = nh - r + 1
    seg = x.reshape(B, nh, CMP_STRIDE * D)
    w1s = w1.reshape(r, CMP_STRIDE * D, -1)
    pre = b1 + pe.reshape(-1) @ w1
    for i in range(r):
        pre = pre + (seg @ w1s[i])[:, i:i + nc]
    return jax.nn.gelu(pre) @ w2


def cmp_to_sel_weights(nc, ns):
    c0 = np.arange(nc)[:, None] * CMP_STRIDE
    s0 = np.arange(ns)[None, :] * SEL_BLOCK
    ov = np.clip(np.minimum(c0 + CMP_BLOCK, s0 + SEL_BLOCK) - np.maximum(c0, s0), 0, None)
    return jnp.asarray(ov / CMP_BLOCK, dtype=jnp.float32)


def nsa_attention(q, gates, rows, win_all, win_start, qpos, pe, w1, b1, w2):
    B, L, _ = rows.shape
    D = NSA_HD
    slopes = alibi_slopes(NSA_H)
    scale = D ** -0.5
    lp = -(-L // SEL_BLOCK) * SEL_BLOCK
    rows = jnp.pad(rows, ((0, 0), (0, lp - L), (0, 0)))
    k_cmp = compress_blocks(rows[..., :D], pe[0], w1[0], b1[0], w2[0])
    v_cmp = compress_blocks(rows[..., D:2 * D], pe[1], w1[1], b1[1], w2[1])
    nc, ns = k_cmp.shape[1], lp // SEL_BLOCK
    cmp_end = jnp.arange(nc, dtype=jnp.int32) * CMP_STRIDE + (CMP_BLOCK - 1)
    cmp_to_sel = cmp_to_sel_weights(nc, ns)
    kv_slc = rows[..., 2 * D:].reshape(B, ns, SEL_BLOCK, 2 * D)
    n_sel = min(N_SEL, ns)
    blk_idx = jnp.arange(ns, dtype=jnp.int32)
    tok = jnp.arange(SEL_BLOCK, dtype=jnp.int32)
    win_pad = jnp.pad(win_all, ((0, 0), (WINDOW, 0), (0, 0)))

    def block(qp, qb, gb):
        nq = qb.shape[1]
        dist_c = qp[:, None] - cmp_end[None, :]
        ok_c = dist_c >= 0
        s_c = jnp.einsum('bqhd,bnd->bhqn', qb, k_cmp).astype(jnp.float32) * scale
        s_c = s_c - slopes[:, None, None] * dist_c
        p_c = jnp.where(ok_c, jax.nn.softmax(jnp.where(ok_c, s_c, NEG), axis=-1), 0.0)
        o_c = jnp.einsum('bhqn,bnd->bqhd', p_c.astype(v_cmp.dtype), v_cmp)
        imp = jnp.einsum('bhqn,nj->bqj', p_c, cmp_to_sel)
        cur = (qp // SEL_BLOCK)[:, None]
        ok_s = blk_idx[None, :] * SEL_BLOCK <= qp[:, None]
        forced = (blk_idx[None, :] == 0) | (blk_idx[None, :] == cur) | (blk_idx[None, :] == cur - 1)
        score = jnp.where(ok_s, jnp.where(forced, FORCE, imp), -FORCE)
        _, idx = lax.top_k(score, n_sel)
        kv_g = jax.vmap(lambda kvb, ib: kvb[ib])(kv_slc, idx)
        dist_s = qp[None, :, None, None] - (idx[..., None] * SEL_BLOCK + tok)
        s_s = jnp.einsum('bqhd,bqnkd->bhqnk', qb, kv_g[..., :D]).astype(jnp.float32) * scale
        s_s = jnp.where(dist_s[:, None] >= 0, s_s - slopes[None, :, None, None, None] * dist_s[:, None], NEG)
        p_s = jax.nn.softmax(s_s.reshape(B, NSA_H, nq, -1), axis=-1).reshape(s_s.shape)
        o_s = jnp.einsum('bhqnk,bqnkd->bqhd', p_s.astype(kv_g.dtype), kv_g[..., D:])
        wlen = WINDOW + nq
        kv_w = lax.dynamic_slice_in_dim(win_pad, qp[0] - win_start, wlen, axis=1)
        pos_w = qp[0] - WINDOW + jnp.arange(wlen, dtype=jnp.int32)
        dist_w = qp[:, None] - pos_w[None, :]
        ok_w = (dist_w >= 0) & (dist_w <= WINDOW) & (pos_w[None, :] >= 0)
        s_w = jnp.einsum('bqhd,bkd->bhqk', qb, kv_w[..., :D]).astype(jnp.float32) * scale
        p_w = jax.nn.softmax(jnp.where(ok_w, s_w - slopes[:, None, None] * dist_w, NEG), axis=-1)
        o_w = jnp.einsum('bhqk,bkd->bqhd', p_w.astype(kv_w.dtype), kv_w[..., D:])
        return gb[..., 0:1] * o_c + gb[..., 1:2] * o_s + gb[..., 2:3] * o_w

    return map_query_blocks(block, qpos, q, gates)


def trunk_layer(x, qpos, past_diff, past_mla, past_nsa, win_buf, mem_kv, layer, w):
    B, T, _ = x.shape
    h = rmsnorm(x, w['ln_mix'])
    cuts = [int(c) for c in np.cumsum(IN_SPLITS)[:-1]]
    d_q, d_kv, m_cq, m_ckv, m_kr, n_q, n_kv, n_win, n_g, br_g = jnp.split(h @ w['w_in'], cuts, axis=-1)

    kv = jnp.concatenate([past_diff, d_kv], axis=1)
    L = kv.shape[1]
    kpos = jnp.arange(L, dtype=jnp.int32)
    lam_init = 0.8 - 0.6 * math.exp(-0.3 * layer)
    lp = w['diff_lambda'].astype(jnp.float32)
    lam = jnp.exp(jnp.sum(lp[0] * lp[1])) - jnp.exp(jnp.sum(lp[2] * lp[3])) + lam_init
    o = diff_attention(d_q.reshape(B, T, DIFF_H, 2, DIFF_HD),
                       kv[..., :2 * DIFF_HD].reshape(B, L, 2, DIFF_HD), kv[..., 2 * DIFF_HD:],
                       qpos, kpos, lam)
    o_diff = (rmsnorm(o, w['diff_subln']) * (1.0 - lam_init)).reshape(B, T, BRANCH_W)

    q = (rmsnorm(m_cq, w['mla_q_norm']) @ w['mla_w_uq']).reshape(B, T, MLA_H, MLA_NOPE + MLA_ROPE)
    q_lat = jnp.einsum('bthn,chn->bthc', q[..., :MLA_NOPE], w['mla_w_uk'])
    q_rope = rope(q[..., MLA_NOPE:], qpos)
    mla_rows = jnp.concatenate([rmsnorm(m_ckv, w['mla_kv_norm']), rope(m_kr, qpos)], axis=-1)
    lat = jnp.concatenate([past_mla, mla_rows], axis=1)
    o_lat = mla_attention(q_lat, q_rope, lat[..., :MLA_KV_LORA], lat[..., MLA_KV_LORA:], qpos, kpos)
    o_mla = jnp.einsum('bthc,chv->bthv', o_lat, w['mla_w_uv']).reshape(B, T, BRANCH_W)

    full = jnp.concatenate([past_nsa, n_kv], axis=1)
    win_all = jnp.concatenate([win_buf, n_win], axis=1)
    win_start = L - win_all.shape[1]
    gates = jax.nn.sigmoid(n_g.reshape(B, T, NSA_H, 3))
    o_nsa = nsa_attention(n_q.reshape(B, T, NSA_H, NSA_HD), gates, full, win_all, win_start, qpos,
                          w['nsa_cmp_pe'], w['nsa_cmp_w1'], w['nsa_cmp_b1'], w['nsa_cmp_w2'])
    o_nsa = o_nsa.reshape(B, T, BRANCH_W)
    new_win = win_all[:, win_all.shape[1] - min(WINDOW, win_all.shape[1]):]

    branches = jnp.stack([o_diff, o_mla, o_nsa], axis=2)
    proj = jnp.einsum('btnc,ncd->btnd', branches, w['w_branch'])
    merged = jnp.sum(jax.nn.sigmoid(br_g.reshape(B, T, N_BRANCH, D_MODEL)) * proj, axis=2)
    x = x + merged @ w['w_out']

    hq = (rmsnorm(x, w['ln_xattn']) @ w['mem_wq']).reshape(B, T, MEM_H, MEM_HD)
    mk = mem_kv[..., :MEM_H * MEM_HD].reshape(B, -1, MEM_H, MEM_HD)
    mv = mem_kv[..., MEM_H * MEM_HD:].reshape(B, -1, MEM_H, MEM_HD)
    s = jnp.einsum('bthd,bmhd->bhtm', hq, mk).astype(jnp.float32) * (MEM_HD ** -0.5)
    p = jax.nn.softmax(s, axis=-1).astype(mv.dtype)
    x = x + jnp.einsum('bhtm,bmhd->bthd', p, mv).reshape(B, T, MEM_H * MEM_HD) @ w['mem_wo']

    g, u = jnp.split(rmsnorm(x, w['ln_ffn']) @ w['ffn_w_gu'], 2, axis=-1)
    x = x + (jax.nn.silu(g) * u) @ w['ffn_w_down']
    return x, d_kv, mla_rows, n_kv, new_win


def setup_inputs(seed: int = 0) -> dict:
    key = jax.random.key(seed)
    ks = iter(jax.random.split(key, 48))
    f32 = jnp.float32

    def nrm(shape, scale):
        return jax.random.normal(next(ks), shape, f32) * scale

    def gain(shape):
        return 1.0 + 0.02 * jax.random.normal(next(ks), shape, f32)

    n_pages = PAST_LEN // PAGE_SIZE
    n_used = DEC_BATCH * n_pages
    n_pool = n_used + max(1, n_used // 4)
    win_buf = min(WINDOW, PAST_LEN)
    x_prompt = jax.random.normal(next(ks), (BATCH, SEQ, D_MODEL), f32)
    x_sample = jax.random.normal(next(ks), (DEC_BATCH, DEC_SEQ, D_MODEL), f32)
    mem_prompt = jax.random.normal(next(ks), (BATCH, N_MEM, D_MODEL), f32)
    cache_diff = jax.random.normal(next(ks), (n_pool, DEPTH, PAGE_SIZE, DIFF_KV_W), f32)
    cache_mla = jax.random.normal(next(ks), (n_pool, DEPTH, PAGE_SIZE, MLA_KV_LORA + MLA_ROPE), f32)
    cache_nsa = jax.random.normal(next(ks), (n_pool, DEPTH, PAGE_SIZE, 4 * NSA_HD), f32)
    state_nsa_win = jax.random.normal(next(ks), (DEC_BATCH, DEPTH, win_buf, 2 * NSA_HD), f32)
    cache_mem = jax.random.normal(next(ks), (DEC_BATCH, DEPTH, N_MEM, 2 * MEM_H * MEM_HD), f32)
    page_table = jax.random.permutation(next(ks), n_pool)[:n_used].reshape(DEC_BATCH, n_pages).astype(jnp.int32)
    return {
        'x_prompt': x_prompt, 'x_sample': x_sample, 'mem_prompt': mem_prompt,
        'cache_diff': cache_diff, 'cache_mla': cache_mla, 'cache_nsa': cache_nsa,
        'state_nsa_win': state_nsa_win, 'cache_mem': cache_mem, 'page_table': page_table,
        'ln_mix': gain((DEPTH, D_MODEL)),
        'w_in': nrm((DEPTH, D_MODEL, N_IN), D_MODEL ** -0.5),
        'diff_lambda': nrm((DEPTH, 4, DIFF_HD), 0.1),
        'diff_subln': gain((DEPTH, 2 * DIFF_HD)),
        'mla_q_norm': gain((DEPTH, MLA_Q_LORA)),
        'mla_kv_norm': gain((DEPTH, MLA_KV_LORA)),
        'mla_w_uq': nrm((DEPTH, MLA_Q_LORA, MLA_H * (MLA_NOPE + MLA_ROPE)), MLA_Q_LORA ** -0.5),
        'mla_w_uk': nrm((DEPTH, MLA_KV_LORA, MLA_H, MLA_NOPE), MLA_KV_LORA ** -0.5),
        'mla_w_uv': nrm((DEPTH, MLA_KV_LORA, MLA_H, MLA_V), MLA_KV_LORA ** -0.5),
        'nsa_cmp_pe': nrm((DEPTH, 2, CMP_BLOCK, NSA_HD), 0.1),
        'nsa_cmp_w1': nrm((DEPTH, 2, CMP_BLOCK * NSA_HD, CMP_HIDDEN), (CMP_BLOCK * NSA_HD) ** -0.5),
        'nsa_cmp_b1': nrm((DEPTH, 2, CMP_HIDDEN), 0.01),
        'nsa_cmp_w2': nrm((DEPTH, 2, CMP_HIDDEN, NSA_HD), CMP_HIDDEN ** -0.5),
        'w_branch': nrm((DEPTH, N_BRANCH, BRANCH_W, D_MODEL), BRANCH_W ** -0.5),
        'w_out': nrm((DEPTH, D_MODEL, D_MODEL), D_MODEL ** -0.5),
        'ln_xattn': gain((DEPTH, D_MODEL)),
        'ln_mem': gain((DEPTH, D_MODEL)),
        'mem_wq': nrm((DEPTH, D_MODEL, MEM_H * MEM_HD), D_MODEL ** -0.5),
        'mem_wkv': nrm((DEPTH, D_MODEL, 2 * MEM_H * MEM_HD), D_MODEL ** -0.5),
        'mem_wo': nrm((DEPTH, MEM_H * MEM_HD, D_MODEL), (MEM_H * MEM_HD) ** -0.5),
        'ln_ffn': gain((DEPTH, D_MODEL)),
        'ffn_w_gu': nrm((DEPTH, D_MODEL, 2 * D_FF), D_MODEL ** -0.5),
        'ffn_w_down': nrm((DEPTH, D_FF, D_MODEL), D_FF ** -0.5),
        'ln_final': gain((D_MODEL,)),
    }


def reference(x_prompt, x_sample, mem_prompt, cache_diff, cache_mla, cache_nsa, state_nsa_win,
              cache_mem, page_table, ln_mix, w_in, diff_lambda, diff_subln, mla_q_norm, mla_kv_norm,
              mla_w_uq, mla_w_uk, mla_w_uv, nsa_cmp_pe, nsa_cmp_w1, nsa_cmp_b1, nsa_cmp_w2,
              w_branch, w_out, ln_xattn, ln_mem, mem_wq, mem_wkv, mem_wo, ln_ffn, ffn_w_gu,
              ffn_w_down, ln_final):
    B, S, _ = x_prompt.shape
    DB, T, _ = x_sample.shape
    past_len = page_table.shape[1] * cache_diff.shape[2]
    qpos_p = jnp.arange(S, dtype=jnp.int32)
    qpos_s = past_len + jnp.arange(T, dtype=jnp.int32)

    def empty(width):
        return jnp.zeros((B, 0, width), x_prompt.dtype)

    def gather_past(cache, l):
        rows = cache[page_table, l]
        return rows.reshape(rows.shape[0], -1, rows.shape[-1])

    xp, xs = x_prompt, x_sample
    dp, ds, mp, ms, np_, ns_, wp, ws, memp = [], [], [], [], [], [], [], [], []
    for l in range(DEPTH):
        w = dict(ln_mix=ln_mix[l], w_in=w_in[l], diff_lambda=diff_lambda[l], diff_subln=diff_subln[l],
                 mla_q_norm=mla_q_norm[l], mla_kv_norm=mla_kv_norm[l], mla_w_uq=mla_w_uq[l],
                 mla_w_uk=mla_w_uk[l], mla_w_uv=mla_w_uv[l], nsa_cmp_pe=nsa_cmp_pe[l],
                 nsa_cmp_w1=nsa_cmp_w1[l], nsa_cmp_b1=nsa_cmp_b1[l], nsa_cmp_w2=nsa_cmp_w2[l],
                 w_branch=w_branch[l], w_out=w_out[l], ln_xattn=ln_xattn[l], mem_wq=mem_wq[l],
                 mem_wo=mem_wo[l], ln_ffn=ln_ffn[l], ffn_w_gu=ffn_w_gu[l], ffn_w_down=ffn_w_down[l])
        mem_kv_p = rmsnorm(mem_prompt, ln_mem[l]) @ mem_wkv[l]
        xp, r_d, r_m, r_n, r_w = trunk_layer(xp, qpos_p, empty(DIFF_KV_W), empty(MLA_KV_LORA + MLA_ROPE),
                                             empty(4 * NSA_HD), empty(2 * NSA_HD), mem_kv_p, l, w)
        xs, s_d, s_m, s_n, s_w = trunk_layer(xs, qpos_s, gather_past(cache_diff, l), gather_past(cache_mla, l),
                                             gather_past(cache_nsa, l), state_nsa_win[:, l], cache_mem[:, l], l, w)
        dp.append(r_d); ds.append(s_d); mp.append(r_m); ms.append(s_m)
        np_.append(r_n); ns_.append(s_n); wp.append(r_w); ws.append(s_w); memp.append(mem_kv_p)

    y_prompt = rmsnorm(xp, ln_final)
    y_sample = rmsnorm(xs, ln_final)
    return (y_prompt, y_sample,
            jnp.stack(dp, axis=1), jnp.stack(ds, axis=1),
            jnp.stack(mp, axis=1), jnp.stack(ms, axis=1),
            jnp.stack(np_, axis=1), jnp.stack(ns_, axis=1),
            jnp.stack(wp, axis=1), jnp.stack(ws, axis=1),
            jnp.stack(memp, axis=1))
```

```python
import functools
import math

import numpy as np
import jax
import jax.numpy as jnp
from jax import lax
from jax.experimental import pallas as pl
from jax.experimental.pallas import tpu as pltpu

F32 = jnp.float32
BF16 = jnp.bfloat16

D_MODEL = 1024
PAGE_SIZE = 128
DIFF_HD = 64
DIFF_H = 4
MLA_H = 8
MLA_NOPE = 64
MLA_ROPE = 32
MLA_V = 64
MLA_Q_LORA = 256
MLA_KV_LORA = 128
ROPE_THETA = 10000.0
NSA_H = 8
NSA_HD = 64
CMP_BLOCK = 32
CMP_STRIDE = 16
CMP_HIDDEN = 128
SEL_BLOCK = 64
N_SEL = 16
WINDOW = 512
MEM_H = 4
MEM_HD = 64
D_FF = 2816
N_BRANCH = 3
NEG = -1e30
FORCE = 1e6
EPS = 1e-6
LANE = 128
VMEM_LIMIT = 56 * 1024 * 1024

OFF_DQ = 0
OFF_DKV = OFF_DQ + 1024
OFF_MCQ = OFF_DKV + 256
OFF_MCKV = OFF_MCQ + 256
OFF_MKR = OFF_MCKV + 128
OFF_MKRS = OFF_MKR + 128
OFF_NQ = OFF_MKRS + 128
OFF_NKV = OFF_NQ + 1024
OFF_NWIN = OFF_NKV + 256
OFF_NG = OFF_NWIN + 128
OFF_BRG = OFF_NG + 128
N_PROJ = OFF_BRG + N_BRANCH * D_MODEL


def _dot(a, b):
    return jnp.dot(a, b, preferred_element_type=F32)


def _dot_nt(a, b):
    return lax.dot_general(a, b, (((1,), (1,)), ((), ())), preferred_element_type=F32)


def _rms(x, g):
    return x * lax.rsqrt(jnp.mean(x * x, axis=-1, keepdims=True) + EPS) * g


def _sigmoid(x):
    return 1.0 / (1.0 + jnp.exp(-x))


def _iota(shape, dim):
    return lax.broadcasted_iota(jnp.int32, shape, dim)


def _params(n_grid):
    return pltpu.CompilerParams(dimension_semantics=("arbitrary",) * n_grid,
                                vmem_limit_bytes=VMEM_LIMIT)


def _full(shape):
    n = len(shape)
    return pl.BlockSpec(shape, lambda *a: (0,) * n)


def _alibi(n, h):
    return float(2.0 ** (-8.0 * (h + 1) / n))


def _proj_in_kernel(x_ref, g_ref, w_ref, gq_ref, gkv_ref, wn_ref, wuk_ref, wr_ref, wrs_ref,
                    cos_ref, sin_ref,
                    dq_ref, dkv_ref, dkvb_ref, qmla_ref, mrow_ref, kvmla_ref, nq_ref,
                    nkv_ref, slcb_ref, nwin_ref, nwinb_ref, gate_ref, brg_ref):
    x = x_ref[...]
    h = _rms(x, g_ref[...]).astype(BF16)

    def piece(off, width):
        return _dot(h, w_ref[:, off:off + width])

    dq_ref[...] = piece(OFF_DQ, 1024).astype(BF16)
    dkv = piece(OFF_DKV, 256)
    dkv_ref[...] = dkv
    dkvb_ref[...] = dkv.astype(BF16)

    cos = cos_ref[...]
    sin = sin_ref[...]
    cqn = _rms(piece(OFF_MCQ, 256), gq_ref[...]).astype(BF16)
    q_nope = _dot(cqn, wn_ref[...]).astype(BF16)
    q_lat = _dot(q_nope, wuk_ref[...])
    q_r = _dot(cqn, wr_ref[...])
    q_rs = _dot(cqn, wrs_ref[...])
    for hh in range(MLA_H):
        sl = slice(hh * LANE, (hh + 1) * LANE)
        qmla_ref[:, hh * 256:hh * 256 + LANE] = q_lat[:, sl].astype(BF16)
        qmla_ref[:, hh * 256 + LANE:(hh + 1) * 256] = (q_r[:, sl] * cos + q_rs[:, sl] * sin).astype(BF16)
    ckv = _rms(piece(OFF_MCKV, 128), gkv_ref[...])
    kr2 = piece(OFF_MKR, 256)
    kr = kr2[:, :LANE] * cos + kr2[:, LANE:] * sin
    mrow_ref[:, :MLA_KV_LORA] = ckv
    mrow_ref[:, MLA_KV_LORA:] = kr[:, :MLA_ROPE]
    kvmla_ref[:, :LANE] = ckv.astype(BF16)
    kvmla_ref[:, LANE:] = kr.astype(BF16)
    nq_ref[...] = piece(OFF_NQ, 1024).astype(BF16)
    nkv = piece(OFF_NKV, 256)
    nkv_ref[...] = nkv
    slcb_ref[...] = nkv[:, LANE:].astype(BF16)
    nwin = piece(OFF_NWIN, 128)
    nwin_ref[...] = nwin
    nwinb_ref[...] = nwin.astype(BF16)
    gate_ref[...] = _sigmoid(piece(OFF_NG, 128))
    for c in range(3):
        brg_ref[:, c * 1024:(c + 1) * 1024] = _sigmoid(piece(OFF_BRG + c * 1024, 1024))


def _proj_in(x, lw, cos_t, sin_t, n_pos_blocks):
    m = x.shape[0]
    tm = min(256, m)
    grid = (m // tm,)

    def row(width):
        return pl.BlockSpec((tm, width), lambda i: (i, 0))

    pos_spec = pl.BlockSpec((tm, LANE), lambda i: (i % n_pos_blocks, 0))
    in_specs = [row(D_MODEL), _full((1, D_MODEL)), _full((D_MODEL, N_PROJ)),
                _full((1, MLA_Q_LORA)), _full((1, MLA_KV_LORA)),
                _full((MLA_Q_LORA, 512)), _full((512, 1024)),
                _full((MLA_Q_LORA, 1024)), _full((MLA_Q_LORA, 1024)),
                pos_spec, pos_spec]
    outs = [(1024, BF16), (256, F32), (256, BF16), (2048, BF16), (160, F32), (256, BF16),
            (1024, BF16), (256, F32), (128, BF16), (128, F32), (128, BF16), (128, F32),
            (3072, F32)]
    return pl.pallas_call(
        _proj_in_kernel,
        grid=grid,
        in_specs=in_specs,
        out_specs=[row(w) for w, _ in outs],
        out_shape=[jax.ShapeDtypeStruct((m, w), dt) for w, dt in outs],
        compiler_params=_params(1),
        name="proj_in",
    )(x, lw["ln_mix"], lw["w_all"], lw["gq"], lw["gkv"], lw["wn"], lw["wuk_bd"],
      lw["wr"], lw["wrs"], cos_t, sin_t)


def _init_state(m_ref, l_ref, acc_ref):
    m_ref[...] = jnp.full(m_ref.shape, NEG, F32)
    l_ref[...] = jnp.zeros(l_ref.shape, F32)
    acc_ref[...] = jnp.zeros(acc_ref.shape, F32)


def _online_update(idx, s, v, m_ref, l_ref, acc_ref):
    m_prev = m_ref[idx]
    m_new = jnp.maximum(m_prev, jnp.max(s, axis=1, keepdims=True))
    alpha = jnp.exp(m_prev - m_new)
    p = jnp.exp(s - m_new)
    l_ref[idx] = alpha * l_ref[idx] + jnp.sum(p, axis=1, keepdims=True)
    acc_ref[idx] = alpha * acc_ref[idx] + _dot(p.astype(BF16), v)
    m_ref[idx] = m_new


def _rel_dist(q0, k0, tq, tk):
    return ((q0 - k0) + _iota((tq, tk), 0) - _iota((tq, tk), 1)).astype(F32)


def _diff_lambda(lp_ref, lam_init):
    lp = lp_ref[...]
    a = jnp.sum(lp[0:1] * lp[1:2], axis=(0, 1), keepdims=True)
    b = jnp.sum(lp[2:3] * lp[3:4], axis=(0, 1), keepdims=True)
    return jnp.exp(a) - jnp.exp(b) + lam_init


def _diff_prefill_kernel(q_ref, kv_ref, lp_ref, sg_ref, o_ref, m_ref, l_ref, acc_ref, *, t, lam_init):
    i = pl.program_id(1)
    scale = DIFF_HD ** -0.5
    _init_state(m_ref, l_ref, acc_ref)

    def tile(j, masked):
        k0 = pl.multiple_of(j * t, t)
        kv = kv_ref[0, pl.ds(k0, t), :]
        k12 = kv[:, :LANE]
        v = kv[:, LANE:]
        dist = _rel_dist(i * t, j * t, t, t)
        ok = dist >= 0.0
        for hh in range(DIFF_H):
            bias = _alibi(DIFF_H, hh) * dist
            for c in range(2):
                sl = hh * 2 + c
                s = _dot_nt(q_ref[0, :, sl * LANE:(sl + 1) * LANE], k12) * scale - bias
                if masked:
                    s = jnp.where(ok, s, NEG)
                _online_update(sl, s, v, m_ref, l_ref, acc_ref)

    def body(j, carry):
        tile(j, False)
        return carry

    lax.fori_loop(0, i, body, 0)
    tile(i, True)

    lam = _diff_lambda(lp_ref, lam_init)
    for hh in range(DIFF_H):
        o1 = acc_ref[2 * hh] / l_ref[2 * hh]
        o2 = acc_ref[2 * hh + 1] / l_ref[2 * hh + 1]
        o = o1 - lam * o2
        o_ref[0, :, hh * LANE:(hh + 1) * LANE] = (_rms(o, sg_ref[...]) * (1.0 - lam_init)).astype(BF16)


def _diff_prefill(dq, dkvb, lp, sg, lam_init, t):
    b, s, _ = dq.shape
    kern = functools.partial(_diff_prefill_kernel, t=t, lam_init=lam_init)
    return pl.pallas_call(
        kern,
        grid=(b, s // t),
        in_specs=[pl.BlockSpec((1, t, 1024), lambda bb, i: (bb, i, 0)),
                  pl.BlockSpec((1, s, 256), lambda bb, i: (bb, 0, 0)),
                  _full((4, DIFF_HD)), _full((1, 2 * DIFF_HD))],
        out_specs=pl.BlockSpec((1, t, 512), lambda bb, i: (bb, i, 0)),
        out_shape=jax.ShapeDtypeStruct((b, s, 512), BF16),
        scratch_shapes=[pltpu.VMEM((8, t, 1), F32), pltpu.VMEM((8, t, 1), F32),
                        pltpu.VMEM((8, t, LANE), F32)],
        compiler_params=_params(2),
        name="diff_prefill",
    )(dq, dkvb, lp, sg)


def _mla_prefill_kernel(q_ref, kv_ref, wuv_ref, o_ref, m_ref, l_ref, acc_ref, *, t):
    i = pl.program_id(1)
    scale = (MLA_NOPE + MLA_ROPE) ** -0.5
    _init_state(m_ref, l_ref, acc_ref)

    def tile(j, masked):
        k0 = pl.multiple_of(j * t, t)
        kv = kv_ref[0, pl.ds(k0, t), :]
        v = kv[:, :LANE]
        if masked:
            ok = _rel_dist(i * t, j * t, t, t) >= 0.0
        for hh in range(MLA_H):
            s = _dot_nt(q_ref[0, :, hh * 256:(hh + 1) * 256], kv) * scale
            if masked:
                s = jnp.where(ok, s, NEG)
            _online_update(hh, s, v, m_ref, l_ref, acc_ref)

    def body(j, carry):
        tile(j, False)
        return carry

    lax.fori_loop(0, i, body, 0)
    tile(i, True)

    o_lat = jnp.concatenate([(acc_ref[hh] / l_ref[hh]).astype(BF16) for hh in range(MLA_H)], axis=1)
    o_ref[0] = _dot(o_lat, wuv_ref[...]).astype(BF16)


def _mla_prefill(qmla, kvmla, wuv_bd, t):
    b, s, _ = qmla.shape
    return pl.pallas_call(
        functools.partial(_mla_prefill_kernel, t=t),
        grid=(b, s // t),
        in_specs=[pl.BlockSpec((1, t, 2048), lambda bb, i: (bb, i, 0)),
                  pl.BlockSpec((1, s, 256), lambda bb, i: (bb, 0, 0)),
                  _full((1024, 512))],
        out_specs=pl.BlockSpec((1, t, 512), lambda bb, i: (bb, i, 0)),
        out_shape=jax.ShapeDtypeStruct((b, s, 512), BF16),
        scratch_shapes=[pltpu.VMEM((8, t, 1), F32), pltpu.VMEM((8, t, 1), F32),
                        pltpu.VMEM((8, t, LANE), F32)],
        compiler_params=_params(2),
        name="mla_prefill",
    )(qmla, kvmla, wuv_bd)


def _gelu_tanh(x):
    return 0.5 * x * (1.0 + jnp.tanh(math.sqrt(2.0 / math.pi) * (x + 0.044715 * (x * x * x))))


def _compress(load_rows, n_seg, wexp_ref, pe_ref, b1_ref, w2_ref):
    ab = jnp.zeros((n_seg, 4 * CMP_HIDDEN), F32)
    cacc = jnp.zeros((16, 4 * CMP_HIDDEN), F32)
    for r in range(CMP_STRIDE):
        w = wexp_ref[r]
        ab = ab + _dot(load_rows(r).astype(BF16), w)
        cacc = cacc + _dot(pe_ref[r].astype(BF16), w)
    half = 2 * CMP_HIDDEN
    const = b1_ref[...] + cacc[0:1, :half] + cacc[1:2, half:]
    pre = const + ab[:, :half] + pltpu.roll(ab[:, half:], n_seg - 1, 0)
    return _dot(_gelu_tanh(pre).astype(BF16), w2_ref[...])


def _compress_prefill_kernel(x_ref, wexp_ref, pe_ref, b1_ref, w2_ref, o_ref, *, n_seg):
    def load_rows(r):
        return x_ref[0, pl.ds(r, n_seg, stride=CMP_STRIDE), :]

    o_ref[0] = _compress(load_rows, n_seg, wexp_ref, pe_ref, b1_ref, w2_ref).astype(BF16)


def _compress_prefill(nkv, lw):
    b, s, _ = nkv.shape
    n_seg = s // CMP_STRIDE
    return pl.pallas_call(
        functools.partial(_compress_prefill_kernel, n_seg=n_seg),
        grid=(b,),
        in_specs=[pl.BlockSpec((1, s, LANE), lambda bb: (bb, 0, 0)),
                  _full((CMP_STRIDE, LANE, 512)), _full((CMP_STRIDE, 16, LANE)),
                  _full((1, 256)), _full((256, LANE))],
        out_specs=pl.BlockSpec((1, n_seg, LANE), lambda bb: (bb, 0, 0)),
        out_shape=jax.ShapeDtypeStruct((b, n_seg, LANE), BF16),
        compiler_params=_params(1),
        name="nsa_compress",
    )(nkv, lw["cmp_wexp"], lw["cmp_pe"], lw["cmp_b1"], lw["cmp_w2"])


def _cmp_branch(q_slabs, kvcmp, qpos, c2s_ref, n_valid_sel, n_heads_rows):
    ncp = kvcmp.shape[0]
    rows = n_heads_rows
    cmp_end = _iota((1, ncp), 1) * CMP_STRIDE + (CMP_BLOCK - 1)
    dist_i = qpos - cmp_end
    ok = dist_i >= 0
    dist = dist_i.astype(F32)
    scale = NSA_HD ** -0.5
    outs = []
    p_sum = jnp.zeros((rows, ncp), F32)
    for hh in range(NSA_H):
        s = _dot_nt(q_slabs[hh], kvcmp) * scale - _alibi(NSA_H, hh) * dist
        s = jnp.where(ok, s, NEG)
        e = jnp.exp(s - jnp.max(s, axis=1, keepdims=True))
        p = jnp.where(ok, e / jnp.sum(e, axis=1, keepdims=True), 0.0)
        outs.append(_dot(p.astype(BF16), kvcmp))
        p_sum = p_sum + p
    p_hi = p_sum.astype(BF16)
    p_lo = (p_sum - p_hi.astype(F32)).astype(BF16)
    c2s = c2s_ref[...]
    imp = _dot(p_hi, c2s) + _dot(p_lo, c2s)
    return outs, imp


def _select_blocks(imp, qpos, ns, n_sel):
    rows, nsp = imp.shape
    blk = _iota((1, nsp), 1)
    cur = lax.shift_right_logical(qpos, int(math.log2(SEL_BLOCK)))
    ok_s = blk * SEL_BLOCK <= qpos
    forced = (blk == 0) | (blk == cur) | (blk == cur - 1)
    score = jnp.where(ok_s, jnp.where(forced, FORCE, imp), -FORCE)
    score = jnp.where(blk < ns, score, -jnp.inf)
    blk_f = blk.astype(F32)
    sel = jnp.zeros((rows, nsp), F32)
    idxs = []
    for _ in range(n_sel):
        mx = jnp.max(score, axis=1, keepdims=True)
        idx = jnp.min(jnp.where(score == mx, blk_f, float(nsp)), axis=1, keepdims=True)
        hit = blk_f == idx
        sel = jnp.where(hit, 1.0, sel)
        score = jnp.where(hit, -jnp.inf, score)
        idxs.append(idx)
    return sel, idxs


def _nsa_prefill_kernel(q_ref, cmp_ref, slc_ref, win_ref, gate_ref, c2s_ref, o_ref,
                        m_ref, l_ref, acc_ref, *, t, ns, n_sel):
    i = pl.program_id(1)
    scale = NSA_HD ** -0.5
    q0 = i * t
    qpos = q0 + _iota((t, 1), 0)
    q_slabs = [q_ref[0, :, hh * LANE:(hh + 1) * LANE] for hh in range(NSA_H)]

    o_c, imp = _cmp_branch(q_slabs, cmp_ref[0], qpos, c2s_ref, ns, t)
    sel, _ = _select_blocks(imp, qpos, ns, n_sel)
    sel_b = sel.astype(BF16)
    gates = gate_ref[0]

    def gate(hh, g):
        c = hh * 3 + g
        return gates[:, c:c + 1]

    for hh in range(NSA_H):
        o_ref[0, :, hh * LANE:(hh + 1) * LANE] = (gate(hh, 0) * o_c[hh]).astype(BF16)

    _init_state(m_ref, l_ref, acc_ref)
    nsp = sel.shape[1]

    def slc_tile(j, carry):
        k0 = pl.multiple_of(j * t, t)
        kv = slc_ref[0, pl.ds(k0, t), :]
        kblk = lax.shift_right_logical(k0 + _iota((nsp, t), 1), int(math.log2(SEL_BLOCK)))
        expand = jnp.where(kblk == _iota((nsp, t), 0), 1.0, 0.0).astype(BF16)
        member = _dot(sel_b, expand)
        dist = _rel_dist(q0, k0, t, t)
        ok = jnp.where(dist >= 0.0, member, 0.0) > 0.5
        for hh in range(NSA_H):
            s = _dot_nt(q_slabs[hh], kv) * scale - _alibi(NSA_H, hh) * dist
            _online_update(hh, jnp.where(ok, s, NEG), kv, m_ref, l_ref, acc_ref)
        return carry

    lax.fori_loop(0, i + 1, slc_tile, 0)
    for hh in range(NSA_H):
        sl = slice(hh * LANE, (hh + 1) * LANE)
        o_s = acc_ref[hh] / l_ref[hh]
        o_ref[0, :, sl] = (o_ref[0, :, sl].astype(F32) + gate(hh, 1) * o_s).astype(BF16)

    _init_state(m_ref, l_ref, acc_ref)
    n_band = -(-WINDOW // t) + 1
    for back in range(n_band):
        j_raw = i - back
        j = jnp.maximum(j_raw, 0)
        k0 = pl.multiple_of(j * t, t)
        kv = win_ref[0, pl.ds(k0, t), :]
        dist = _rel_dist(q0, k0, t, t)
        dist_chk = dist + jnp.where(j_raw >= 0, 0.0, 2.0 * WINDOW + t)
        ok = jnp.where(dist >= 0.0, dist_chk, 2.0 * WINDOW) <= float(WINDOW)
        for hh in range(NSA_H):
            s = _dot_nt(q_slabs[hh], kv) * scale - _alibi(NSA_H, hh) * dist
            _online_update(hh, jnp.where(ok, s, NEG), kv, m_ref, l_ref, acc_ref)
    for hh in range(NSA_H):
        sl = slice(hh * LANE, (hh + 1) * LANE)
        o_w = acc_ref[hh] / l_ref[hh]
        o_ref[0, :, sl] = (o_ref[0, :, sl].astype(F32) + gate(hh, 2) * o_w).astype(BF16)


def _nsa_prefill(nq, kvcmp, slcb, nwinb, gates, c2s, t):
    b, s, _ = nq.shape
    ns = s // SEL_BLOCK
    ncp = kvcmp.shape[1]
    nsp = c2s.shape[1]
    kern = functools.partial(_nsa_prefill_kernel, t=t, ns=ns, n_sel=min(N_SEL, ns))
    return pl.pallas_call(
        kern,
        grid=(b, s // t),
        in_specs=[pl.BlockSpec((1, t, 1024), lambda bb, i: (bb, i, 0)),
                  pl.BlockSpec((1, ncp, LANE), lambda bb, i: (bb, 0, 0)),
                  pl.BlockSpec((1, s, LANE), lambda bb, i: (bb, 0, 0)),
                  pl.BlockSpec((1, s, LANE), lambda bb, i: (bb, 0, 0)),
                  pl.BlockSpec((1, t, LANE), lambda bb, i: (bb, i, 0)),
                  _full((ncp, nsp))],
        out_specs=pl.BlockSpec((1, t, 1024), lambda bb, i: (bb, i, 0)),
        out_shape=jax.ShapeDtypeStruct((b, s, 1024), BF16),
        scratch_shapes=[pltpu.VMEM((8, t, 1), F32), pltpu.VMEM((8, t, 1), F32),
                        pltpu.VMEM((8, t, LANE), F32)],
        compiler_params=_params(2),
        name="nsa_prefill",
    )(nq, kvcmp, slcb, nwinb, gates, c2s)


def _merge_kernel(x_ref, od_ref, om_ref, on_ref, brg_ref, wb0_ref, wb1_ref, wb2_ref, wo_ref,
                  gx_ref, wq_ref, x1_ref, hq_ref):
    merged = brg_ref[:, 0:1024] * _dot(od_ref[...].astype(BF16), wb0_ref[...])
    merged = merged + brg_ref[:, 1024:2048] * _dot(om_ref[...].astype(BF16), wb1_ref[...])
    merged = merged + brg_ref[:, 2048:3072] * _dot(on_ref[...].astype(BF16), wb2_ref[...])
    x1 = x_ref[...] + _dot(merged.astype(BF16), wo_ref[...])
    x1_ref[...] = x1
    hq_ref[...] = _dot(_rms(x1, gx_ref[...]).astype(BF16), wq_ref[...]).astype(BF16)


def _merge(x, o_diff, o_mla, o_nsa, brg, lw):
    m = x.shape[0]
    tm = min(256, m)

    def row(width):
        return pl.BlockSpec((tm, width), lambda i: (i, 0))

    return pl.pallas_call(
        _merge_kernel,
        grid=(m // tm,),
        in_specs=[row(1024), row(512), row(512), row(1024), row(3072),
                  _full((512, 1024)), _full((512, 1024)), _full((1024, 1024)), _full((1024, 1024)),
                  _full((1, 1024)), _full((1024, 1024))],
        out_specs=[row(1024), row(1024)],
        out_shape=[jax.ShapeDtypeStruct((m, 1024), F32), jax.ShapeDtypeStruct((m, 1024), BF16)],
        compiler_params=_params(1),
        name="merge_out",
    )(x, o_diff, o_mla, o_nsa, brg, lw["wb0"], lw["wb1"], lw["wb2"], lw["w_out"],
      lw["ln_xattn"], lw["mem_wq_bd"])


def _xattn_rows(hq, mem):
    mk = mem[:, :256]
    mv = mem[:, 256:]
    lane_head = lax.shift_right_logical(_iota((1, 256), 1), int(math.log2(MEM_HD)))
    out = jnp.zeros((hq.shape[0], 256), F32)
    for hh in range(MEM_H):
        s = _dot_nt(hq[:, hh * 256:(hh + 1) * 256], mk)
        e = jnp.exp(s - jnp.max(s, axis=1, keepdims=True))
        p = e / jnp.sum(e, axis=1, keepdims=True)
        out = out + jnp.where(lane_head == hh, _dot(p.astype(BF16), mv), 0.0)
    return out


def _xattn_prefill_kernel(hq_ref, mem_ref, o_ref):
    o_ref[0] = _xattn_rows(hq_ref[0], mem_ref[0].astype(BF16)).astype(BF16)


def _xattn_prefill(hq, mem_kv, t):
    b, s, _ = hq.shape
    n_mem = mem_kv.shape[1]
    return pl.pallas_call(
        _xattn_prefill_kernel,
        grid=(b, s // t),
        in_specs=[pl.BlockSpec((1, t, 1024), lambda bb, i: (bb, i, 0)),
                  pl.BlockSpec((1, n_mem, 512), lambda bb, i: (bb, 0, 0))],
        out_specs=pl.BlockSpec((1, t, 256), lambda bb, i: (bb, i, 0)),
        out_shape=jax.ShapeDtypeStruct((b, s, 256), BF16),
        compiler_params=_params(2),
        name="xattn_prefill",
    )(hq, mem_kv)


def _xattn_decode_kernel(hq_ref, mem_ref, o_ref, *, group, layer):
    hq_all = hq_ref[...].astype(F32)
    for g in range(group):
        hq = jnp.broadcast_to(hq_all[g:g + 1, :], (8, 1024)).astype(BF16)
        o = _xattn_rows(hq, mem_ref[g, 0].astype(BF16))
        o_ref[g:g + 1, :] = o[0:1]


def _xattn_decode(hq, cache_mem, layer):
    db = hq.shape[0]
    n_mem = cache_mem.shape[2]
    group = 8
    return pl.pallas_call(
        functools.partial(_xattn_decode_kernel, group=group, layer=layer),
        grid=(db // group,),
        in_specs=[pl.BlockSpec((group, 1024), lambda i: (i, 0)),
                  pl.BlockSpec((group, 1, n_mem, 512), lambda i: (i, layer, 0, 0))],
        out_specs=pl.BlockSpec((group, 256), lambda i: (i, 0)),
        out_shape=jax.ShapeDtypeStruct((db, 256), F32),
        compiler_params=_params(1),
        name="xattn_decode",
    )(hq, cache_mem)


def _ffn_kernel(x_ref, a_ref, wo_ref, g_ref, wgu_ref, wd_ref, gf_ref, o_ref, *, final, chunk):
    x2 = x_ref[...] + _dot(a_ref[...].astype(BF16), wo_ref[...])
    hn = _rms(x2, g_ref[...]).astype(BF16)
    acc = x2
    for c in range(D_FF // chunk):
        g = _dot(hn, wgu_ref[:, c * chunk:(c + 1) * chunk])
        u = _dot(hn, wgu_ref[:, D_FF + c * chunk:D_FF + (c + 1) * chunk])
        act = (g * _sigmoid(g) * u).astype(BF16)
        acc = acc + _dot(act, wd_ref[c * chunk:(c + 1) * chunk, :])
    if final:
        acc = _rms(acc, gf_ref[...])
    o_ref[...] = acc


def _ffn(x1, attn, lw, ln_final, final):
    m = x1.shape[0]
    tm = min(256, m)
    chunk = 1408
    assert D_FF % chunk == 0

    def row(width):
        return pl.BlockSpec((tm, width), lambda i: (i, 0))

    return pl.pallas_call(
        functools.partial(_ffn_kernel, final=final, chunk=chunk),
        grid=(m // tm,),
        in_specs=[row(1024), row(256), _full((256, 1024)), _full((1, 1024)),
                  _full((1024, 2 * D_FF)), _full((D_FF, 1024)), _full((1, 1024))],
        out_specs=row(1024),
        out_shape=jax.ShapeDtypeStruct((m, 1024), F32),
        compiler_params=_params(1),
        name="ffn",
    )(x1, attn, lw["mem_wo"], lw["ln_ffn"], lw["w_gu"], lw["w_down"], ln_final)


def _memkv_kernel(x_ref, g_ref, w_ref, o_ref):
    o_ref[...] = _dot(_rms(x_ref[...], g_ref[...]).astype(BF16), w_ref[...])


def _memkv(mem, g, w):
    m = mem.shape[0]
    tm = min(256, m)
    return pl.pallas_call(
        _memkv_kernel,
        grid=(m // tm,),
        in_specs=[pl.BlockSpec((tm, 1024), lambda i: (i, 0)), _full((1, 1024)), _full((1024, 512))],
        out_specs=pl.BlockSpec((tm, 512), lambda i: (i, 0)),
        out_shape=jax.ShapeDtypeStruct((m, 512), F32),
        compiler_params=_params(1),
        name="mem_kv",
    )(mem, g, w)


def _page_specs(n_pages, width_block, col_block, layer):
    def spec(p):
        return pl.BlockSpec((1, 1, PAGE_SIZE, width_block),
                            lambda b, pt: (pt[b * n_pages + p], layer, 0, col_block))
    return [spec(p) for p in range(n_pages)]


def _softmax_with_new(s, s_new):
    m = jnp.maximum(jnp.max(s, axis=1, keepdims=True), s_new)
    p = jnp.exp(s - m)
    p_new = jnp.exp(s_new - m)
    denom = jnp.sum(p, axis=1, keepdims=True) + p_new
    return p, p_new, denom


def _diff_decode_kernel(pt_ref, q_ref, new_ref, slope_ref, lp_ref, sg_ref, *rest,
                        n_pages, lam_init):
    pages = rest[:n_pages]
    o_ref = rest[n_pages]
    b = pl.program_id(0)
    scale = DIFF_HD ** -0.5
    past = n_pages * PAGE_SIZE
    q = q_ref[0]
    kvs = [pg[0, 0].astype(BF16) for pg in pages]
    s = jnp.concatenate([_dot_nt(q, kv[:, :LANE]) for kv in kvs], axis=1)
    dist = (past - _iota((1, past), 1)).astype(F32)
    slope = slope_ref[...]
    s = s * scale - slope * dist
    new = new_ref[pl.ds(b % 8, 1), :]
    s_new = jnp.sum(q.astype(F32) * new[:, :LANE], axis=1, keepdims=True) * scale
    p, p_new, denom = _softmax_with_new(s, s_new)
    pb = p.astype(BF16)
    acc = p_new * new[:, LANE:]
    for n, kv in enumerate(kvs):
        acc = acc + _dot(pb[:, n * PAGE_SIZE:(n + 1) * PAGE_SIZE], kv[:, LANE:])
    o = acc / denom
    lam = _diff_lambda(lp_ref, lam_init)
    for hh in range(DIFF_H):
        oh = o[2 * hh:2 * hh + 1] - lam * o[2 * hh + 1:2 * hh + 2]
        o_ref[0, hh:hh + 1, :] = _rms(oh, sg_ref[...]) * (1.0 - lam_init)


def _diff_decode(pt_flat, q3, new_b, slopes, lp, sg, cache, layer, lam_init, n_pages):
    db = q3.shape[0]
    kern = functools.partial(_diff_decode_kernel, n_pages=n_pages, lam_init=lam_init)
    grid_spec = pltpu.PrefetchScalarGridSpec(
        num_scalar_prefetch=1,
        grid=(db,),
        in_specs=[pl.BlockSpec((1, 8, LANE), lambda b, pt: (b, 0, 0)),
                  pl.BlockSpec((8, 256), lambda b, pt: (b // 8, 0)),
                  pl.BlockSpec((8, 1), lambda b, pt: (0, 0)),
                  pl.BlockSpec((4, DIFF_HD), lambda b, pt: (0, 0)),
                  pl.BlockSpec((1, 2 * DIFF_HD), lambda b, pt: (0, 0))]
        + _page_specs(n_pages, 256, 0, layer),
        out_specs=pl.BlockSpec((1, DIFF_H, LANE), lambda b, pt: (b, 0, 0)),
    )
    return pl.pallas_call(
        kern,
        grid_spec=grid_spec,
        out_shape=jax.ShapeDtypeStruct((db, DIFF_H, LANE), F32),
        compiler_params=_params(1),
        name="diff_decode",
    )(pt_flat, q3, new_b, slopes, lp, sg, *([cache] * n_pages))


def _mla_decode_kernel(pt_ref, q_ref, new_ref, wuv_ref, *rest, n_pages):
    pages = rest[:n_pages]
    o_ref = rest[n_pages]
    kv_ref = rest[n_pages + 1]
    b = pl.program_id(0)
    scale = (MLA_NOPE + MLA_ROPE) ** -0.5
    width = MLA_KV_LORA + MLA_ROPE
    kv_ref[:, width:] = jnp.zeros((n_pages * PAGE_SIZE, 256 - width), BF16)
    for n, pg in enumerate(pages):
        kv_ref[n * PAGE_SIZE:(n + 1) * PAGE_SIZE, :width] = pg[0, 0].astype(BF16)
    q = q_ref[0]
    kv = kv_ref[...]
    s = _dot_nt(q, kv) * scale
    new = new_ref[pl.ds(b % 8, 1), :]
    s_new = jnp.sum(q.astype(F32) * new, axis=1, keepdims=True) * scale
    p, p_new, denom = _softmax_with_new(s, s_new)
    acc = p_new * new[:, :LANE] + _dot(p.astype(BF16), kv[:, :LANE])
    o_lat = acc / denom
    out = jnp.zeros((1, 512), F32)
    for hh in range(MLA_H):
        out = out + _dot(jnp.broadcast_to(o_lat[hh:hh + 1], (8, LANE)).astype(BF16),
                         wuv_ref[hh * LANE:(hh + 1) * LANE, :])[0:1]
    o_ref[0] = out


def _mla_decode(pt_flat, q3, new_b, wuv_bd, cache, layer, n_pages):
    db = q3.shape[0]
    width = MLA_KV_LORA + MLA_ROPE
    grid_spec = pltpu.PrefetchScalarGridSpec(
        num_scalar_prefetch=1,
        grid=(db,),
        in_specs=[pl.BlockSpec((1, 8, 256), lambda b, pt: (b, 0, 0)),
                  pl.BlockSpec((8, 256), lambda b, pt: (b // 8, 0)),
                  pl.BlockSpec((1024, 512), lambda b, pt: (0, 0))]
        + _page_specs(n_pages, width, 0, layer),
        out_specs=pl.BlockSpec((1, 1, 512), lambda b, pt: (b, 0, 0)),
        scratch_shapes=[pltpu.VMEM((n_pages * PAGE_SIZE, 256), BF16)],
    )
    return pl.pallas_call(
        functools.partial(_mla_decode_kernel, n_pages=n_pages),
        grid_spec=grid_spec,
        out_shape=jax.ShapeDtypeStruct((db, 1, 512), F32),
        compiler_params=_params(1),
        name="mla_decode",
    )(pt_flat, q3, new_b, wuv_bd, *([cache] * n_pages))


def _nsa_decode_cmp_kernel(pt_ref, q_ref, wexp_ref, pe_ref, b1_ref, w2_ref, c2s_ref, *rest,
                           n_pages, ns, n_sel):
    pages = rest[:n_pages]
    oc_ref, sel_ref = rest[n_pages], rest[n_pages + 1]
    seg_per_page = PAGE_SIZE // CMP_STRIDE
    n_seg = n_pages * seg_per_page
    past = n_pages * PAGE_SIZE

    def load_rows(r):
        return jnp.concatenate([pg[0, 0, pl.ds(r, seg_per_page, stride=CMP_STRIDE), :] for pg in pages],
                               axis=0)

    kvcmp = _compress(load_rows, n_seg, wexp_ref, pe_ref, b1_ref, w2_ref).astype(BF16)
    q = q_ref[0].astype(F32)
    qpos = jnp.full((8, 1), past, jnp.int32)
    q_slabs = [jnp.broadcast_to(q[:, hh * LANE:(hh + 1) * LANE], (8, LANE)).astype(BF16)
               for hh in range(NSA_H)]
    o_c, imp = _cmp_branch(q_slabs, kvcmp, qpos, c2s_ref, ns, 8)
    for hh in range(NSA_H):
        oc_ref[0, :, hh * LANE:(hh + 1) * LANE] = o_c[hh][0:1]
    _, idxs = _select_blocks(imp[0:1], qpos[0:1], ns, n_sel)
    lane = _iota((1, LANE), 1)
    sel = jnp.zeros((1, LANE), F32)
    for k, idx in enumerate(idxs):
        sel = jnp.where(lane == k, idx, sel)
    sel_ref[0] = sel.astype(jnp.int32)


def _nsa_decode_cmp(pt_flat, nq3, lw, c2s, cache, layer, n_pages):
    db = nq3.shape[0]
    ns = (n_pages * PAGE_SIZE + 1 + SEL_BLOCK - 1) // SEL_BLOCK
    n_sel = min(N_SEL, ns)
    ncp, nsp = c2s.shape
    grid_spec = pltpu.PrefetchScalarGridSpec(
        num_scalar_prefetch=1,
        grid=(db,),
        in_specs=[pl.BlockSpec((1, 1, 1024), lambda b, pt: (b, 0, 0)),
                  pl.BlockSpec((CMP_STRIDE, LANE, 512), lambda b, pt: (0, 0, 0)),
                  pl.BlockSpec((CMP_STRIDE, 16, LANE), lambda b, pt: (0, 0, 0)),
                  pl.BlockSpec((1, 256), lambda b, pt: (0, 0)),
                  pl.BlockSpec((256, LANE), lambda b, pt: (0, 0)),
                  pl.BlockSpec((ncp, nsp), lambda b, pt: (0, 0))]
        + _page_specs(n_pages, LANE, 0, layer),
        out_specs=[pl.BlockSpec((1, 1, 1024), lambda b, pt: (b, 0, 0)),
                   pl.BlockSpec((1, 1, LANE), lambda b, pt: (b, 0, 0))],
    )
    return pl.pallas_call(
        functools.partial(_nsa_decode_cmp_kernel, n_pages=n_pages, ns=ns, n_sel=n_sel),
        grid_spec=grid_spec,
        out_shape=[jax.ShapeDtypeStruct((db, 1, 1024), F32),
                   jax.ShapeDtypeStruct((db, 1, LANE), jnp.int32)],
        compiler_params=_params(1),
        name="nsa_decode_cmp",
    )(pt_flat, nq3, lw["cmp_wexp"], lw["cmp_pe"], lw["cmp_b1"], lw["cmp_w2"], c2s,
      *([cache] * n_pages))


def _nsa_decode_sel_kernel(pg_ref, hf_ref, ix_ref, q_ref, newkv_ref, neww_ref, gate_ref, slope_ref,
                           oc_ref, win_ref, *rest, n_sel, past):
    blocks = rest[:n_sel]
    o_ref, nwin_ref = rest[n_sel], rest[n_sel + 1]
    kv_ref = rest[n_sel + 2]
    b = pl.program_id(0)
    scale = NSA_HD ** -0.5
    row = pl.ds(b % 8, 1)
    q = q_ref[0].astype(F32)
    qf = jnp.concatenate([q[:, hh * LANE:(hh + 1) * LANE] for hh in range(NSA_H)], axis=0)
    qs = qf.astype(BF16)
    slope = slope_ref[...]
    gates = gate_ref[row, :]

    cur_blk = past // SEL_BLOCK
    pos_rows = []
    n_new = jnp.int32(0)
    for k in range(n_sel):
        kv_ref[k * SEL_BLOCK:(k + 1) * SEL_BLOCK, :] = blocks[k][0, 0].astype(BF16)
        idx = ix_ref[b * n_sel + k]
        idx_pos = jnp.where(idx < cur_blk, idx, cur_blk + 1)
        pos_rows.append(idx_pos * SEL_BLOCK + _iota((1, SEL_BLOCK), 1))
        n_new = n_new + (idx == cur_blk).astype(jnp.int32)
    kv = kv_ref[...]
    dist_i = past - jnp.concatenate(pos_rows, axis=1)
    ok = dist_i >= 0
    s = _dot_nt(qs, kv) * scale - slope * dist_i.astype(F32)
    s = jnp.where(ok, s, NEG)
    newkv = newkv_ref[row, :]
    s_new = jnp.sum(qf * newkv, axis=1, keepdims=True) * scale + jnp.where(n_new > 0, 0.0, NEG)
    p, p_new, denom = _softmax_with_new(s, s_new)
    o_s = (p_new * newkv + _dot(p.astype(BF16), kv)) / denom

    win = win_ref[0, 0]
    n_win = win.shape[0]
    winb = win.astype(BF16)
    dist_w = (n_win - _iota((1, n_win), 1)).astype(F32)
    ok_w = dist_w <= float(WINDOW)
    s_w = jnp.where(ok_w, _dot_nt(qs, winb) * scale - slope * dist_w, NEG)
    neww = neww_ref[row, :]
    s_wn = jnp.sum(qf * neww, axis=1, keepdims=True) * scale
    pw, pw_new, denom_w = _softmax_with_new(s_w, s_wn)
    o_w = (pw_new * neww + _dot(pw.astype(BF16), winb)) / denom_w

    lane = _iota((8, LANE), 1)
    head3 = _iota((8, LANE), 0) * 3
    def gate(g):
        return jnp.sum(jnp.where(lane == head3 + g, gates, 0.0), axis=1, keepdims=True)
    o_c = jnp.concatenate([oc_ref[0, :, hh * LANE:(hh + 1) * LANE] for hh in range(NSA_H)], axis=0)
    o = gate(0) * o_c + gate(1) * o_s + gate(2) * o_w
    for hh in range(NSA_H):
        o_ref[0, :, hh * LANE:(hh + 1) * LANE] = o[hh:hh + 1]

    rolled = pltpu.roll(win, n_win - 1, 0)
    nwin_ref[0, 0] = jnp.where(_iota((n_win, LANE), 0) == n_win - 1, neww, rolled)


def _nsa_decode_sel(sel_pg, sel_hf, sel_ix, nq3, slc_new, win_new, gates, slopes, o_c, state_win, cache,
                    layer, past):
    db = nq3.shape[0]
    n_sel = sel_ix.shape[0] // db
    n_win = state_win.shape[2]

    def blk_spec(k):
        return pl.BlockSpec((1, 1, SEL_BLOCK, LANE),
                            lambda b, pg, hf, ix: (pg[b * n_sel + k], layer, hf[b * n_sel + k], 1))

    def c3(b, pg, hf, ix):
        return (b, 0, 0)

    def g8(b, pg, hf, ix):
        return (b // 8, 0)

    grid_spec = pltpu.PrefetchScalarGridSpec(
        num_scalar_prefetch=3,
        grid=(db,),
        in_specs=[pl.BlockSpec((1, 1, 1024), c3),
                  pl.BlockSpec((8, LANE), g8), pl.BlockSpec((8, LANE), g8), pl.BlockSpec((8, LANE), g8),
                  pl.BlockSpec((8, 1), lambda b, pg, hf, ix: (0, 0)),
                  pl.BlockSpec((1, 1, 1024), c3),
                  pl.BlockSpec((1, 1, n_win, LANE), lambda b, pg, hf, ix: (b, layer, 0, 0))]
        + [blk_spec(k) for k in range(n_sel)],
        out_specs=[pl.BlockSpec((1, 1, 1024), c3),
                   pl.BlockSpec((1, 1, n_win, LANE), lambda b, pg, hf, ix: (b, 0, 0, 0))],
        scratch_shapes=[pltpu.VMEM((n_sel * SEL_BLOCK, LANE), BF16)],
    )
    return pl.pallas_call(
        functools.partial(_nsa_decode_sel_kernel, n_sel=n_sel, past=past),
        grid_spec=grid_spec,
        out_shape=[jax.ShapeDtypeStruct((db, 1, 1024), F32),
                   jax.ShapeDtypeStruct((db, 1, n_win, LANE), F32)],
        compiler_params=_params(1),
        name="nsa_decode_sel",
    )(sel_pg, sel_hf, sel_ix, nq3, slc_new, win_new, gates, slopes, o_c, state_win, *([cache] * n_sel))


def _pad_cols(w, width):
    return jnp.pad(w, ((0, 0), (0, width - w.shape[1])))


def _block_diag(blocks):
    n, r, c = blocks.shape
    eye = jnp.eye(n, dtype=blocks.dtype)
    return (blocks[:, :, None, :] * eye[:, None, :, None]).reshape(n * r, n * c)


def _layer_weights(l, p):
    w = p["w_in"][l]
    cuts = np.cumsum([0, 512, 256, 256, 128, 32, 512, 256, 128, 24, 3072])
    d_q, d_kv, m_cq, m_ckv, m_kr, n_q, n_kv, n_win, n_g, br_g = [w[:, cuts[k]:cuts[k + 1]] for k in range(10)]
    eye2 = jnp.eye(2, dtype=F32)
    dq_bd = (d_q.reshape(D_MODEL, DIFF_H, 2, 1, DIFF_HD) * eye2[None, None, :, :, None]).reshape(D_MODEL, 1024)
    half = MLA_ROPE // 2
    m_kr_sw = jnp.concatenate([m_kr[:, half:], m_kr[:, :half]], axis=1)
    nq_pad = jnp.pad(n_q.reshape(D_MODEL, NSA_H, NSA_HD), ((0, 0), (0, 0), (0, LANE - NSA_HD))).reshape(D_MODEL, 1024)
    w_all = jnp.concatenate([dq_bd, d_kv, m_cq, m_ckv, _pad_cols(m_kr, LANE), _pad_cols(m_kr_sw, LANE),
                             nq_pad, n_kv, n_win, _pad_cols(n_g, LANE), br_g], axis=1).astype(BF16)
    assert w_all.shape[1] == N_PROJ

    wuq = p["mla_w_uq"][l].reshape(MLA_Q_LORA, MLA_H, MLA_NOPE + MLA_ROPE)
    wn = wuq[:, :, :MLA_NOPE].reshape(MLA_Q_LORA, MLA_H * MLA_NOPE)
    wr = wuq[:, :, MLA_NOPE:]
    wr_sw = jnp.concatenate([wr[:, :, half:], wr[:, :, :half]], axis=2)
    pad3 = ((0, 0), (0, 0), (0, LANE - MLA_ROPE))
    wr_p = jnp.pad(wr, pad3).reshape(MLA_Q_LORA, MLA_H * LANE)
    wrs_p = jnp.pad(wr_sw, pad3).reshape(MLA_Q_LORA, MLA_H * LANE)
    wuk_bd = _block_diag(jnp.transpose(p["mla_w_uk"][l], (1, 2, 0)))
    wuv_bd = _block_diag(jnp.transpose(p["mla_w_uv"][l], (1, 0, 2)))

    w1 = p["nsa_cmp_w1"][l].reshape(2, 2, CMP_STRIDE, NSA_HD, CMP_HIDDEN)
    zero = jnp.zeros((CMP_STRIDE, NSA_HD, CMP_HIDDEN), F32)
    top = jnp.concatenate([w1[0, 0], zero, w1[0, 1], zero], axis=2)
    bot = jnp.concatenate([zero, w1[1, 0], zero, w1[1, 1]], axis=2)
    cmp_wexp = jnp.concatenate([top, bot], axis=1).astype(BF16)
    pe = p["nsa_cmp_pe"][l]
    pe_rows = jnp.concatenate([pe[0], pe[1]], axis=1)
    cmp_pe = jnp.zeros((CMP_STRIDE, 16, LANE), F32)
    cmp_pe = cmp_pe.at[:, 0].set(pe_rows[:CMP_STRIDE]).at[:, 1].set(pe_rows[CMP_STRIDE:])
    cmp_b1 = p["nsa_cmp_b1"][l].reshape(1, 2 * CMP_HIDDEN)
    cmp_w2 = _block_diag(p["nsa_cmp_w2"][l]).astype(BF16)

    wb = p["w_branch"][l]
    wb2 = jnp.pad(wb[2].reshape(NSA_H, NSA_HD, D_MODEL), ((0, 0), (LANE - NSA_HD, 0), (0, 0))).reshape(1024, D_MODEL)
    wq4 = p["mem_wq"][l].reshape(D_MODEL, MEM_H, MEM_HD) * (MEM_HD ** -0.5)
    eye4 = jnp.eye(MEM_H, dtype=F32)
    mem_wq_bd = (wq4[:, :, None, :] * eye4[None, :, :, None]).reshape(D_MODEL, 1024)
    return dict(
        ln_mix=p["ln_mix"][l][None], w_all=w_all, gq=p["mla_q_norm"][l][None], gkv=p["mla_kv_norm"][l][None],
        wn=wn.astype(BF16), wuk_bd=wuk_bd.astype(BF16), wr=wr_p.astype(BF16), wrs=wrs_p.astype(BF16),
        wuv_bd=wuv_bd.astype(BF16), lp=p["diff_lambda"][l], sg=p["diff_subln"][l][None],
        cmp_wexp=cmp_wexp, cmp_pe=cmp_pe, cmp_b1=cmp_b1, cmp_w2=cmp_w2,
        wb0=wb[0].astype(BF16), wb1=wb[1].astype(BF16), wb2=wb2.astype(BF16),
        w_out=p["w_out"][l].astype(BF16), ln_xattn=p["ln_xattn"][l][None], mem_wq_bd=mem_wq_bd.astype(BF16),
        mem_wo=p["mem_wo"][l].astype(BF16), ln_ffn=p["ln_ffn"][l][None],
        w_gu=p["ffn_w_gu"][l].astype(BF16), w_down=p["ffn_w_down"][l].astype(BF16),
        ln_mem=p["ln_mem"][l][None], mem_wkv=p["mem_wkv"][l].astype(BF16),
    )


def _rope_tables(pos):
    half = MLA_ROPE // 2
    inv = ROPE_THETA ** (-np.arange(half, dtype=np.float32) / half)
    ang = pos.astype(F32)[:, None] * jnp.asarray(inv, F32)
    cos, sin = jnp.cos(ang), jnp.sin(ang)
    zeros = jnp.zeros((pos.shape[0], LANE - MLA_ROPE), F32)
    return jnp.concatenate([cos, cos, zeros], axis=1), jnp.concatenate([-sin, sin, zeros], axis=1)


def _cmp_to_sel(ncp, nsp):
    c0 = np.arange(ncp)[:, None] * CMP_STRIDE
    s0 = np.arange(nsp)[None, :] * SEL_BLOCK
    ov = np.clip(np.minimum(c0 + CMP_BLOCK, s0 + SEL_BLOCK) - np.maximum(c0, s0), 0, None)
    return jnp.asarray(ov / CMP_BLOCK, dtype=BF16)


def _round_up(n, m):
    return -(-n // m) * m


def kernel(x_prompt, x_sample, mem_prompt, cache_diff, cache_mla, cache_nsa, state_nsa_win, cache_mem, page_table, ln_mix, w_in, diff_lambda, diff_subln, mla_q_norm, mla_kv_norm, mla_w_uq, mla_w_uk, mla_w_uv, nsa_cmp_pe, nsa_cmp_w1, nsa_cmp_b1, nsa_cmp_w2, w_branch, w_out, ln_xattn, ln_mem, mem_wq, mem_wkv, mem_wo, ln_ffn, ffn_w_gu, ffn_w_down, ln_final):
    p = dict(ln_mix=ln_mix, w_in=w_in, diff_lambda=diff_lambda, diff_subln=diff_subln,
             mla_q_norm=mla_q_norm, mla_kv_norm=mla_kv_norm, mla_w_uq=mla_w_uq, mla_w_uk=mla_w_uk,
             mla_w_uv=mla_w_uv, nsa_cmp_pe=nsa_cmp_pe, nsa_cmp_w1=nsa_cmp_w1, nsa_cmp_b1=nsa_cmp_b1,
             nsa_cmp_w2=nsa_cmp_w2, w_branch=w_branch, w_out=w_out, ln_xattn=ln_xattn, ln_mem=ln_mem,
             mem_wq=mem_wq, mem_wkv=mem_wkv, mem_wo=mem_wo, ln_ffn=ln_ffn, ffn_w_gu=ffn_w_gu,
             ffn_w_down=ffn_w_down)
    bsz, seq, _ = x_prompt.shape
    db = x_sample.shape[0]
    depth = ln_mix.shape[0]
    n_pages = page_table.shape[1]
    past = n_pages * PAGE_SIZE
    n_mem = mem_prompt.shape[1]
    t = min(256, seq)
    assert seq % t == 0 and seq % SEL_BLOCK == 0 and x_sample.shape[1] == 1 and db % 8 == 0

    cos_p, sin_p = _rope_tables(jnp.arange(seq, dtype=jnp.int32))
    cos_s, sin_s = _rope_tables(jnp.full((min(256, db),), past, jnp.int32))
    ncp_p = seq // CMP_STRIDE
    c2s_p = _cmp_to_sel(ncp_p, _round_up(seq // SEL_BLOCK, LANE))
    ncp_s = past // CMP_STRIDE
    ns_s = (past + 1 + SEL_BLOCK - 1) // SEL_BLOCK
    c2s_s = _cmp_to_sel(ncp_s, _round_up(ns_s, LANE))
    pt_flat = page_table.reshape(-1)
    slopes_diff = jnp.asarray(np.repeat([_alibi(DIFF_H, h) for h in range(DIFF_H)], 2)[:, None], F32)
    slopes_nsa = jnp.asarray(np.array([_alibi(NSA_H, h) for h in range(NSA_H)])[:, None], F32)

    xp = x_prompt.reshape(bsz * seq, D_MODEL)
    xs = x_sample.reshape(db, D_MODEL)
    mem_flat = mem_prompt.reshape(bsz * n_mem, D_MODEL)
    outs = {k: [] for k in ("dp", "ds", "mp", "ms", "np", "ns", "wp", "ws", "memp")}
    for l in range(depth):
        lw = _layer_weights(l, p)
        lam_init = 0.8 - 0.6 * math.exp(-0.3 * l)
        final = l == depth - 1
        ln_f = ln_final[None]

        mem_kv = _memkv(mem_flat, lw["ln_mem"], lw["mem_wkv"]).reshape(bsz, n_mem, 512)
        (dq, dkv, dkvb, qmla, mrow, kvmla, nq, nkv, slcb, nwin, nwinb, gates, brg) = _proj_in(
            xp, lw, cos_p, sin_p, seq // t)

        def b3(a):
            return a.reshape(bsz, seq, a.shape[-1])

        o_diff = _diff_prefill(b3(dq), b3(dkvb), lw["lp"], lw["sg"], lam_init, t)
        o_mla = _mla_prefill(b3(qmla), b3(kvmla), lw["wuv_bd"], t)
        kvcmp = _compress_prefill(b3(nkv), lw)
        o_nsa = _nsa_prefill(b3(nq), kvcmp, b3(slcb), b3(nwinb), b3(gates), c2s_p, t)
        x1, hq = _merge(xp, o_diff.reshape(-1, 512), o_mla.reshape(-1, 512), o_nsa.reshape(-1, 1024), brg, lw)
        attn = _xattn_prefill(b3(hq), mem_kv, t).reshape(-1, 256)
        xp = _ffn(x1, attn, lw, ln_f, final)
        outs["dp"].append(b3(dkv)); outs["mp"].append(b3(mrow)); outs["np"].append(b3(nkv))
        outs["wp"].append(b3(nwin)[:, seq - min(WINDOW, seq):]); outs["memp"].append(mem_kv)

        (dq, dkv, dkvb, qmla, mrow, kvmla, nq, nkv, slcb, nwin, nwinb, gates, brg) = _proj_in(
            xs, lw, cos_s, sin_s, 1)
        o_diff = _diff_decode(pt_flat, dq.reshape(db, 8, LANE), dkv, slopes_diff, lw["lp"], lw["sg"],
                              cache_diff, l, lam_init, n_pages).reshape(db, 512)
        o_mla = _mla_decode(pt_flat, qmla.reshape(db, 8, 256), kvmla.astype(F32), lw["wuv_bd"], cache_mla, l,
                            n_pages).reshape(db, 512)
        nq3 = nq.reshape(db, 1, 1024)
        o_c, sel = _nsa_decode_cmp(pt_flat, nq3, lw, c2s_s, cache_nsa, l, n_pages)
        n_sel = min(N_SEL, ns_s)
        sel_ix = sel[:, 0, :n_sel]
        cached_ix = jnp.minimum(sel_ix, past // SEL_BLOCK - 1)
        blocks_per_page = PAGE_SIZE // SEL_BLOCK
        sel_pg = jnp.take_along_axis(page_table, cached_ix // blocks_per_page, axis=1)
        sel_hf = cached_ix % blocks_per_page
        o_nsa, new_win = _nsa_decode_sel(sel_pg.reshape(-1), sel_hf.reshape(-1), sel_ix.reshape(-1), nq3,
                                         nkv[:, LANE:], nwin, gates, slopes_nsa, o_c, state_nsa_win,
                                         cache_nsa, l, past)
        x1, hq = _merge(xs, o_diff, o_mla, o_nsa.reshape(db, 1024), brg, lw)
        attn = _xattn_decode(hq, cache_mem, l)
        xs = _ffn(x1, attn, lw, ln_f, final)
        outs["ds"].append(dkv[:, None]); outs["ms"].append(mrow[:, None]); outs["ns"].append(nkv[:, None])
        outs["ws"].append(new_win[:, 0])

    y_prompt = xp.reshape(bsz, seq, D_MODEL)
    y_sample = xs.reshape(db, 1, D_MODEL)
    st = lambda k: jnp.stack(outs[k], axis=1)
    return (y_prompt, y_sample, st("dp"), st("ds"), st("mp"), st("ms"), st("np"), st("ns"),
            st("wp"), st("ws"), st("memp"))
```

```python
import functools
import math

import numpy as np
import jax
import jax.numpy as jnp
from jax import lax
from jax.experimental import pallas as pl
from jax.experimental.pallas import tpu as pltpu

F32 = jnp.float32
BF16 = jnp.bfloat16

D_MODEL = 1024
PAGE_SIZE = 128
DIFF_HD = 64
DIFF_H = 4
MLA_H = 8
MLA_NOPE = 64
MLA_ROPE = 32
MLA_V = 64
MLA_Q_LORA = 256
MLA_KV_LORA = 128
ROPE_THETA = 10000.0
NSA_H = 8
NSA_HD = 64
CMP_BLOCK = 32
CMP_STRIDE = 16
CMP_HIDDEN = 128
SEL_BLOCK = 64
SEL_SHIFT = 6
N_SEL = 16
WINDOW = 512
MEM_H = 4
MEM_HD = 64
D_FF = 2816
N_BRANCH = 3
NEG = -1e30
FORCE = 1e6
EPS = 1e-6
LANE = 128
VMEM_LIMIT = 56 * 1024 * 1024

OFF_DQ = 0
OFF_DKV = OFF_DQ + 1024
OFF_MCQ = OFF_DKV + 256
OFF_MCKV = OFF_MCQ + 256
OFF_MKR = OFF_MCKV + 128
OFF_MKRS = OFF_MKR + 128
OFF_NQ = OFF_MKRS + 128
OFF_NKV = OFF_NQ + 1024
OFF_NWIN = OFF_NKV + 256
OFF_NG = OFF_NWIN + 128
OFF_BRG = OFF_NG + 128
N_PROJ = OFF_BRG + N_BRANCH * D_MODEL


def _dot(a, b):
    return jnp.dot(a, b, preferred_element_type=F32)


def _dot_nt(a, b):
    return lax.dot_general(a, b, (((1,), (1,)), ((), ())), preferred_element_type=F32)


def _rms(x, g):
    return x * lax.rsqrt(jnp.mean(x * x, axis=-1, keepdims=True) + EPS) * g


def _sigmoid(x):
    return 1.0 / (1.0 + jnp.exp(-x))


def _iota(shape, dim):
    return lax.broadcasted_iota(jnp.int32, shape, dim)


def _params(n_grid):
    return pltpu.CompilerParams(dimension_semantics=("arbitrary",) * n_grid,
                                vmem_limit_bytes=VMEM_LIMIT)


def _full(shape):
    n = len(shape)
    return pl.BlockSpec(shape, lambda *a: (0,) * n)


def _alibi(n, h):
    return float(2.0 ** (-8.0 * (h + 1) / n))


def _proj_in_kernel(x_ref, g_ref, w_ref, gq_ref, gkv_ref, wn_ref, wuk_ref, wr_ref, wrs_ref,
                    cos_ref, sin_ref,
                    dq_ref, dkv_ref, dkvb_ref, qmla_ref, mrow_ref, kvmla_ref, nq_ref,
                    nkv_ref, slcb_ref, nwin_ref, nwinb_ref, gate_ref, brg_ref, *t_refs):
    x = x_ref[...]
    h = _rms(x, g_ref[...]).astype(BF16)

    def piece(off, width):
        return _dot(h, w_ref[:, off:off + width])

    dq_ref[...] = piece(OFF_DQ, 1024).astype(BF16)
    dkv = piece(OFF_DKV, 256)
    dkv_ref[...] = dkv
    dkvb_ref[...] = dkv.astype(BF16)

    cos = cos_ref[...]
    sin = sin_ref[...]
    cqn = _rms(piece(OFF_MCQ, 256), gq_ref[...]).astype(BF16)
    q_nope = _dot(cqn, wn_ref[...]).astype(BF16)
    q_lat = _dot(q_nope, wuk_ref[...])
    q_r = _dot(cqn, wr_ref[...])
    q_rs = _dot(cqn, wrs_ref[...])
    for hh in range(MLA_H):
        sl = slice(hh * LANE, (hh + 1) * LANE)
        qmla_ref[:, hh * 256:hh * 256 + LANE] = q_lat[:, sl].astype(BF16)
        qmla_ref[:, hh * 256 + LANE:(hh + 1) * 256] = (q_r[:, sl] * cos + q_rs[:, sl] * sin).astype(BF16)
    ckv = _rms(piece(OFF_MCKV, 128), gkv_ref[...])
    kr2 = piece(OFF_MKR, 256)
    kr = kr2[:, :LANE] * cos + kr2[:, LANE:] * sin
    mrow_ref[:, :MLA_KV_LORA] = ckv
    mrow_ref[:, MLA_KV_LORA:] = kr[:, :MLA_ROPE]
    kvmla_ref[:, :LANE] = ckv.astype(BF16)
    kvmla_ref[:, LANE:] = kr.astype(BF16)
    nq_ref[...] = piece(OFF_NQ, 1024).astype(BF16)
    nkv = piece(OFF_NKV, 256)
    nkv_ref[...] = nkv
    slcb_ref[...] = nkv[:, LANE:].astype(BF16)
    nwin = piece(OFF_NWIN, 128)
    nwin_ref[...] = nwin
    nwinb_ref[...] = nwin.astype(BF16)
    gates = _sigmoid(piece(OFF_NG, 128))
    gate_ref[...] = gates
    for c in range(3):
        brg_ref[:, c * 1024:(c + 1) * 1024] = _sigmoid(piece(OFF_BRG + c * 1024, 1024))
    if t_refs:
        dvt_ref, ckvt_ref, slct_ref, wint_ref, gatet_ref = t_refs
        dvt_ref[0] = dkv[:, LANE:].T.astype(BF16)
        ckvt_ref[0] = ckv.T.astype(BF16)
        slct_ref[0] = nkv[:, LANE:].T.astype(BF16)
        wint_ref[0] = nwin.T.astype(BF16)
        gatet_ref[0] = gates.T


def _proj_in(x, lw, cos_t, sin_t, n_pos_blocks, transposed):
    m = x.shape[0]
    tm = min(256, m)
    grid = (m // tm,)

    def row(width):
        return pl.BlockSpec((tm, width), lambda i: (i, 0))

    pos_spec = pl.BlockSpec((tm, LANE), lambda i: (i % n_pos_blocks, 0))
    in_specs = [row(D_MODEL), _full((1, D_MODEL)), _full((D_MODEL, N_PROJ)),
                _full((1, MLA_Q_LORA)), _full((1, MLA_KV_LORA)),
                _full((MLA_Q_LORA, 512)), _full((512, 1024)),
                _full((MLA_Q_LORA, 1024)), _full((MLA_Q_LORA, 1024)),
                pos_spec, pos_spec]
    outs = [(1024, BF16), (256, F32), (256, BF16), (2048, BF16), (160, F32), (256, BF16),
            (1024, BF16), (256, F32), (128, BF16), (128, F32), (128, BF16), (128, F32),
            (3072, F32)]
    out_specs = [row(w) for w, _ in outs]
    out_shape = [jax.ShapeDtypeStruct((m, w), dt) for w, dt in outs]
    if transposed:
        for dt in (BF16, BF16, BF16, BF16, F32):
            out_specs.append(pl.BlockSpec((1, LANE, tm), lambda i: (i, 0, 0)))
            out_shape.append(jax.ShapeDtypeStruct((m // tm, LANE, tm), dt))
    return pl.pallas_call(
        _proj_in_kernel,
        grid=grid,
        in_specs=in_specs,
        out_specs=out_specs,
        out_shape=out_shape,
        compiler_params=_params(1),
        name="proj_in",
    )(x, lw["ln_mix"], lw["w_all"], lw["gq"], lw["gkv"], lw["wn"], lw["wuk_bd"],
      lw["wr"], lw["wrs"], cos_t, sin_t)


def _init_state(m_ref, l_ref, acc_ref):
    m_ref[...] = jnp.full(m_ref.shape, NEG, F32)
    l_ref[...] = jnp.zeros(l_ref.shape, F32)
    acc_ref[...] = jnp.zeros(acc_ref.shape, F32)


def _flash_tile(n, score_fn, shift_fn, vt, state):
    m_ref, l_ref, acc_ref, s_ref, p_ref, a_ref = state

    def scores(sl):
        st = score_fn(sl)
        s_ref[sl] = st
        a_ref[sl] = jnp.max(st, axis=0, keepdims=True)

    def softmax(sl):
        shift = shift_fn(sl)
        m_prev = m_ref[sl]
        m_new = jnp.maximum(m_prev, a_ref[sl] + shift)
        alpha = jnp.exp(m_prev - m_new)
        p = jnp.exp(s_ref[sl] - (m_new - shift))
        l_ref[sl] = alpha * l_ref[sl] + jnp.sum(p, axis=0, keepdims=True)
        m_ref[sl] = m_new
        a_ref[sl] = alpha
        p_ref[sl] = p.astype(BF16)

    def values(sl):
        acc_ref[sl] = a_ref[sl] * acc_ref[sl] + _dot(vt, p_ref[sl])

    for sweep in (scores, softmax, values):
        for sl in range(n):
            sweep(sl)


def _rel_dist_t(q0, k0, t):
    return (q0 - k0) + _iota((t, t), 1) - _iota((t, t), 0)


def _state_scratch(t):
    return [pltpu.VMEM((8, 1, t), F32), pltpu.VMEM((8, 1, t), F32), pltpu.VMEM((8, LANE, t), F32),
            pltpu.VMEM((8, t, t), F32), pltpu.VMEM((8, t, t), BF16), pltpu.VMEM((8, 1, t), F32)]


def _diff_lambda(lp_ref, lam_init):
    lp = lp_ref[...]
    a = jnp.sum(lp[0:1] * lp[1:2], axis=(0, 1), keepdims=True)
    b = jnp.sum(lp[2:3] * lp[3:4], axis=(0, 1), keepdims=True)
    return jnp.exp(a) - jnp.exp(b) + lam_init


def _diff_prefill_kernel(q_ref, kv_ref, vt_ref, lp_ref, sg_ref, o_ref, *state, t, lam_init):
    m_ref, l_ref, acc_ref = state[:3]
    i = pl.program_id(1)
    _init_state(m_ref, l_ref, acc_ref)
    key_in_tile = _iota((t, t), 0).astype(F32)
    biases = [_alibi(DIFF_H, hh) * key_in_tile for hh in range(DIFF_H)]

    def tile(j, masked):
        k0 = pl.multiple_of(j * t, t)
        k12 = kv_ref[0, pl.ds(k0, t), :LANE]
        tile_off = ((j - i) * t).astype(F32)
        if masked:
            ok = _rel_dist_t(0, 0, t) >= 0

        def score(sl):
            st = _dot_nt(k12, q_ref[0, :, sl * LANE:(sl + 1) * LANE]) + biases[sl // 2]
            return jnp.where(ok, st, NEG) if masked else st

        _flash_tile(2 * DIFF_H, score, lambda sl: _alibi(DIFF_H, sl // 2) * tile_off, vt_ref[j], state)

    def body(j, carry):
        tile(j, False)
        return carry

    lax.fori_loop(0, i, body, 0)
    tile(i, True)

    lam = _diff_lambda(lp_ref, lam_init)
    for hh in range(DIFF_H):
        o1 = acc_ref[2 * hh] / l_ref[2 * hh]
        o2 = acc_ref[2 * hh + 1] / l_ref[2 * hh + 1]
        o = (o1 - lam * o2).T
        o_ref[0, :, hh * LANE:(hh + 1) * LANE] = (_rms(o, sg_ref[...]) * (1.0 - lam_init)).astype(BF16)


def _diff_prefill(dq, dkvb, dvt, lp, sg, lam_init, t):
    b, s, _ = dq.shape
    nt = s // t
    kern = functools.partial(_diff_prefill_kernel, t=t, lam_init=lam_init)
    return pl.pallas_call(
        kern,
        grid=(b, nt),
        in_specs=[pl.BlockSpec((1, t, 1024), lambda bb, i: (bb, i, 0)),
                  pl.BlockSpec((1, s, 256), lambda bb, i: (bb, 0, 0)),
                  pl.BlockSpec((nt, LANE, t), lambda bb, i: (bb, 0, 0)),
                  _full((4, DIFF_HD)), _full((1, 2 * DIFF_HD))],
        out_specs=pl.BlockSpec((1, t, 512), lambda bb, i: (bb, i, 0)),
        out_shape=jax.ShapeDtypeStruct((b, s, 512), BF16),
        scratch_shapes=_state_scratch(t),
        compiler_params=_params(2),
        name="diff_prefill",
    )(dq, dkvb, dvt, lp, sg)


def _mla_prefill_kernel(q_ref, kv_ref, vt_ref, wuv_ref, o_ref, *state, t):
    m_ref, l_ref, acc_ref = state[:3]
    i = pl.program_id(1)
    scale = (MLA_NOPE + MLA_ROPE) ** -0.5
    _init_state(m_ref, l_ref, acc_ref)

    def tile(j, masked):
        k0 = pl.multiple_of(j * t, t)
        kv = kv_ref[0, pl.ds(k0, t), :]
        if masked:
            ok = _rel_dist_t(0, 0, t) >= 0

        def score(hh):
            st = _dot_nt(kv, q_ref[0, :, hh * 256:(hh + 1) * 256]) * scale
            return jnp.where(ok, st, NEG) if masked else st

        _flash_tile(MLA_H, score, lambda hh: 0.0, vt_ref[j], state)

    def body(j, carry):
        tile(j, False)
        return carry

    lax.fori_loop(0, i, body, 0)
    tile(i, True)

    o_lat = jnp.concatenate([(acc_ref[hh] / l_ref[hh]).T.astype(BF16) for hh in range(MLA_H)], axis=1)
    o_ref[0] = _dot(o_lat, wuv_ref[...]).astype(BF16)


def _mla_prefill(qmla, kvmla, ckvt, wuv_bd, t):
    b, s, _ = qmla.shape
    nt = s // t
    return pl.pallas_call(
        functools.partial(_mla_prefill_kernel, t=t),
        grid=(b, nt),
        in_specs=[pl.BlockSpec((1, t, 2048), lambda bb, i: (bb, i, 0)),
                  pl.BlockSpec((1, s, 256), lambda bb, i: (bb, 0, 0)),
                  pl.BlockSpec((nt, LANE, t), lambda bb, i: (bb, 0, 0)),
                  _full((1024, 512))],
        out_specs=pl.BlockSpec((1, t, 512), lambda bb, i: (bb, i, 0)),
        out_shape=jax.ShapeDtypeStruct((b, s, 512), BF16),
        scratch_shapes=_state_scratch(t),
        compiler_params=_params(2),
        name="mla_prefill",
    )(qmla, kvmla, ckvt, wuv_bd)


def _gelu_tanh(x):
    return 0.5 * x * (1.0 + jnp.tanh(math.sqrt(2.0 / math.pi) * (x + 0.044715 * (x * x * x))))


def _compress(load_rows, n_seg, wbig_ref, pe_ref, b1_ref, w2_ref):
    seg = jnp.concatenate([load_rows(r).astype(BF16) for r in range(CMP_STRIDE)], axis=1)
    w = wbig_ref[...]
    ab = _dot(seg, w)
    cacc = _dot(pe_ref[...].astype(BF16), w)
    half = 2 * CMP_HIDDEN
    const = b1_ref[...] + cacc[0:1, :half] + cacc[1:2, half:]
    pre = const + ab[:, :half] + pltpu.roll(ab[:, half:], n_seg - 1, 0)
    return _dot(_gelu_tanh(pre).astype(BF16), w2_ref[...])


def _compress_prefill_kernel(x_ref, wbig_ref, pe_ref, b1_ref, w2_ref, o_ref, ot_ref, *, n_seg):
    def load_rows(r):
        return x_ref[0, pl.ds(r, n_seg, stride=CMP_STRIDE), :]

    kv = _compress(load_rows, n_seg, wbig_ref, pe_ref, b1_ref, w2_ref)
    o_ref[0] = kv.astype(BF16)
    ot_ref[0] = kv.T.astype(BF16)


def _compress_prefill(nkv, lw):
    b, s, _ = nkv.shape
    n_seg = s // CMP_STRIDE
    return pl.pallas_call(
        functools.partial(_compress_prefill_kernel, n_seg=n_seg),
        grid=(b,),
        in_specs=[pl.BlockSpec((1, s, LANE), lambda bb: (bb, 0, 0)),
                  _full((CMP_STRIDE * LANE, 512)), _full((16, CMP_STRIDE * LANE)),
                  _full((1, 256)), _full((256, LANE))],
        out_specs=[pl.BlockSpec((1, n_seg, LANE), lambda bb: (bb, 0, 0)),
                   pl.BlockSpec((1, LANE, n_seg), lambda bb: (bb, 0, 0))],
        out_shape=[jax.ShapeDtypeStruct((b, n_seg, LANE), BF16),
                   jax.ShapeDtypeStruct((b, LANE, n_seg), BF16)],
        compiler_params=_params(1),
        name="nsa_compress",
    )(nkv, lw["cmp_wbig"], lw["cmp_pe"], lw["cmp_b1"], lw["cmp_w2"])


def _select_blocks(imp, qpos, ns, n_sel, axis):
    shape = imp.shape
    nsp = shape[axis]
    blk = _iota(shape, axis)
    cur = lax.shift_right_logical(qpos, SEL_SHIFT)
    ok_s = blk * SEL_BLOCK <= qpos
    forced = (blk == 0) | (blk == cur) | (blk == cur - 1)
    score = jnp.where(ok_s, jnp.where(forced, FORCE, imp), -FORCE)
    score = jnp.where(blk < ns, score, -jnp.inf)
    blk_f = blk.astype(F32)
    sel = jnp.zeros(shape, F32)
    idxs = []
    for _ in range(n_sel):
        mx = jnp.max(score, axis=axis, keepdims=True)
        idx = jnp.min(jnp.where(score == mx, blk_f, float(nsp)), axis=axis, keepdims=True)
        hit = blk_f == idx
        sel = jnp.where(hit, 1.0, sel)
        score = jnp.where(hit, -jnp.inf, score)
        idxs.append(idx)
    return sel, idxs


def _nsa_prefill_kernel(q_ref, cmp_ref, cmpt_ref, slc_ref, slct_ref, win_ref, wint_ref, gt_ref, c2st_ref,
                        o_ref, *scratch, t, ns, n_sel):
    state, oacc_ref = scratch[:6], scratch[6]
    m_ref, l_ref, acc_ref = state[:3]
    i = pl.program_id(1)
    q0 = i * t
    qpos = q0 + _iota((1, t), 1)
    q_slabs = [q_ref[0, :, hh * LANE:(hh + 1) * LANE] for hh in range(NSA_H)]
    slopes = [_alibi(NSA_H, hh) for hh in range(NSA_H)]

    def gate(hh, g):
        c = hh * 3 + g
        return gt_ref[0, c:c + 1, :]

    kvcmp = cmp_ref[0]
    kvcmpt = cmpt_ref[0]
    ncp = kvcmp.shape[0]
    cmp_end = _iota((ncp, t), 0) * CMP_STRIDE + (CMP_BLOCK - 1)
    ok_c = (qpos - cmp_end) >= 0
    rel_c = (cmp_end - q0).astype(F32)
    p_sum = jnp.zeros((ncp, t), F32)
    for hh in range(NSA_H):
        st = jnp.where(ok_c, _dot_nt(kvcmp, q_slabs[hh]) + slopes[hh] * rel_c, NEG)
        e = jnp.exp(st - jnp.max(st, axis=0, keepdims=True))
        p = jnp.where(ok_c, e / jnp.sum(e, axis=0, keepdims=True), 0.0)
        oacc_ref[hh] = gate(hh, 0) * _dot(kvcmpt, p.astype(BF16))
        p_sum = p_sum + p
    p_hi = p_sum.astype(BF16)
    p_lo = (p_sum - p_hi.astype(F32)).astype(BF16)
    c2st = c2st_ref[...]
    imp = _dot(c2st, p_hi) + _dot(c2st, p_lo)
    sel, _ = _select_blocks(imp, qpos, ns, n_sel, 0)
    sel_b = sel.astype(BF16)
    nsp = sel.shape[0]

    key_in_tile = _iota((t, t), 0).astype(F32)
    biases = [slopes[hh] * key_in_tile for hh in range(NSA_H)]

    _init_state(m_ref, l_ref, acc_ref)

    def slc_tile(j, carry):
        k0 = pl.multiple_of(j * t, t)
        kv = slc_ref[0, pl.ds(k0, t), :]
        kvt = slct_ref[j]
        kblk = lax.shift_right_logical(k0 + _iota((t, nsp), 0), SEL_SHIFT)
        expand = jnp.where(kblk == _iota((t, nsp), 1), 1.0, 0.0).astype(BF16)
        member = _dot(expand, sel_b)
        ok = jnp.where(_rel_dist_t(q0, k0, t) >= 0, member, 0.0) > 0.5
        tile_off = (k0 - q0).astype(F32)
        _flash_tile(NSA_H, lambda hh: jnp.where(ok, _dot_nt(kv, q_slabs[hh]) + biases[hh], NEG),
                    lambda hh: slopes[hh] * tile_off, kvt, state)
        return carry

    lax.fori_loop(0, i + 1, slc_tile, 0)
    for hh in range(NSA_H):
        oacc_ref[hh] = oacc_ref[hh] + gate(hh, 1) * (acc_ref[hh] / l_ref[hh])

    _init_state(m_ref, l_ref, acc_ref)
    n_band = -(-WINDOW // t) + 1
    for back in range(n_band):
        j_raw = i - back
        j = jnp.maximum(j_raw, 0)
        k0 = pl.multiple_of(j * t, t)
        kv = win_ref[0, pl.ds(k0, t), :]
        kvt = wint_ref[j]
        dist = _rel_dist_t(q0, k0, t)
        dist_chk = dist + jnp.where(j_raw >= 0, 0, 2 * WINDOW + t)
        ok = jnp.where(dist >= 0, dist_chk, 2 * WINDOW) <= WINDOW
        tile_off = (k0 - q0).astype(F32)
        _flash_tile(NSA_H, lambda hh: jnp.where(ok, _dot_nt(kv, q_slabs[hh]) + biases[hh], NEG),
                    lambda hh: slopes[hh] * tile_off, kvt, state)
    for hh in range(NSA_H):
        o = oacc_ref[hh] + gate(hh, 2) * (acc_ref[hh] / l_ref[hh])
        o_ref[0, :, hh * LANE:(hh + 1) * LANE] = o.T.astype(BF16)


def _nsa_prefill(nq, kvcmp, kvcmpt, slcb, slct, nwinb, wint, gatet, c2st, t):
    b, s, _ = nq.shape
    nt = s // t
    ns = s // SEL_BLOCK
    ncp = kvcmp.shape[1]
    nsp = c2st.shape[0]
    kern = functools.partial(_nsa_prefill_kernel, t=t, ns=ns, n_sel=min(N_SEL, ns))
    resident = pl.BlockSpec((1, s, LANE), lambda bb, i: (bb, 0, 0))
    resident_t = pl.BlockSpec((nt, LANE, t), lambda bb, i: (bb, 0, 0))
    return pl.pallas_call(
        kern,
        grid=(b, nt),
        in_specs=[pl.BlockSpec((1, t, 1024), lambda bb, i: (bb, i, 0)),
                  pl.BlockSpec((1, ncp, LANE), lambda bb, i: (bb, 0, 0)),
                  pl.BlockSpec((1, LANE, ncp), lambda bb, i: (bb, 0, 0)),
                  resident, resident_t, resident, resident_t,
                  pl.BlockSpec((1, LANE, t), lambda bb, i: (bb * nt + i, 0, 0)),
                  _full((nsp, ncp))],
        out_specs=pl.BlockSpec((1, t, 1024), lambda bb, i: (bb, i, 0)),
        out_shape=jax.ShapeDtypeStruct((b, s, 1024), BF16),
        scratch_shapes=_state_scratch(t) + [pltpu.VMEM((8, LANE, t), F32)],
        compiler_params=_params(2),
        name="nsa_prefill",
    )(nq, kvcmp, kvcmpt, slcb, slct, nwinb, wint, gatet, c2st)


def _merge_kernel(x_ref, od_ref, om_ref, on_ref, brg_ref, wb0_ref, wb1_ref, wb2_ref, wo_ref,
                  gx_ref, wq_ref, x1_ref, hq_ref):
    merged = brg_ref[:, 0:1024] * _dot(od_ref[...].astype(BF16), wb0_ref[...])
    merged = merged + brg_ref[:, 1024:2048] * _dot(om_ref[...].astype(BF16), wb1_ref[...])
    merged = merged + brg_ref[:, 2048:3072] * _dot(on_ref[...].astype(BF16), wb2_ref[...])
    x1 = x_ref[...] + _dot(merged.astype(BF16), wo_ref[...])
    x1_ref[...] = x1
    hq_ref[...] = _dot(_rms(x1, gx_ref[...]).astype(BF16), wq_ref[...]).astype(BF16)


def _merge(x, o_diff, o_mla, o_nsa, brg, lw):
    m = x.shape[0]
    tm = min(256, m)

    def row(width):
        return pl.BlockSpec((tm, width), lambda i: (i, 0))

    return pl.pallas_call(
        _merge_kernel,
        grid=(m // tm,),
        in_specs=[row(1024), row(512), row(512), row(1024), row(3072),
                  _full((512, 1024)), _full((512, 1024)), _full((1024, 1024)), _full((1024, 1024)),
                  _full((1, 1024)), _full((1024, 1024))],
        out_specs=[row(1024), row(1024)],
        out_shape=[jax.ShapeDtypeStruct((m, 1024), F32), jax.ShapeDtypeStruct((m, 1024), BF16)],
        compiler_params=_params(1),
        name="merge_out",
    )(x, o_diff, o_mla, o_nsa, brg, lw["wb0"], lw["wb1"], lw["wb2"], lw["w_out"],
      lw["ln_xattn"], lw["mem_wq_bd"])


def _xattn_rows(hq, mem):
    mk = mem[:, :256]
    mv = mem[:, 256:]
    lane_head = lax.shift_right_logical(_iota((1, 256), 1), int(math.log2(MEM_HD)))
    out = jnp.zeros((hq.shape[0], 256), F32)
    for hh in range(MEM_H):
        s = _dot_nt(hq[:, hh * 256:(hh + 1) * 256], mk)
        e = jnp.exp(s - jnp.max(s, axis=1, keepdims=True))
        p = e / jnp.sum(e, axis=1, keepdims=True)
        out = out + jnp.where(lane_head == hh, _dot(p.astype(BF16), mv), 0.0)
    return out


def _xattn_prefill_kernel(hq_ref, mem_ref, o_ref):
    o_ref[0] = _xattn_rows(hq_ref[0], mem_ref[0].astype(BF16)).astype(BF16)


def _xattn_prefill(hq, mem_kv, t):
    b, s, _ = hq.shape
    n_mem = mem_kv.shape[1]
    return pl.pallas_call(
        _xattn_prefill_kernel,
        grid=(b, s // t),
        in_specs=[pl.BlockSpec((1, t, 1024), lambda bb, i: (bb, i, 0)),
                  pl.BlockSpec((1, n_mem, 512), lambda bb, i: (bb, 0, 0))],
        out_specs=pl.BlockSpec((1, t, 256), lambda bb, i: (bb, i, 0)),
        out_shape=jax.ShapeDtypeStruct((b, s, 256), BF16),
        compiler_params=_params(2),
        name="xattn_prefill",
    )(hq, mem_kv)


def _xattn_decode_kernel(hq_ref, mem_ref, o_ref, *, group):
    hq_all = hq_ref[...].astype(F32)
    for g in range(group):
        hq = jnp.broadcast_to(hq_all[g:g + 1, :], (8, 1024)).astype(BF16)
        o = _xattn_rows(hq, mem_ref[g, 0].astype(BF16))
        o_ref[g:g + 1, :] = o[0:1]


def _xattn_decode(hq, cache_mem, layer):
    db = hq.shape[0]
    n_mem = cache_mem.shape[2]
    group = 8
    return pl.pallas_call(
        functools.partial(_xattn_decode_kernel, group=group),
        grid=(db // group,),
        in_specs=[pl.BlockSpec((group, 1024), lambda i: (i, 0)),
                  pl.BlockSpec((group, 1, n_mem, 512), lambda i: (i, layer, 0, 0))],
        out_specs=pl.BlockSpec((group, 256), lambda i: (i, 0)),
        out_shape=jax.ShapeDtypeStruct((db, 256), F32),
        compiler_params=_params(1),
        name="xattn_decode",
    )(hq, cache_mem)


def _ffn_kernel(x_ref, a_ref, wo_ref, g_ref, wgu_ref, wd_ref, gf_ref, o_ref, *, final, chunk):
    x2 = x_ref[...] + _dot(a_ref[...].astype(BF16), wo_ref[...])
    hn = _rms(x2, g_ref[...]).astype(BF16)
    acc = x2
    for c in range(D_FF // chunk):
        g = _dot(hn, wgu_ref[:, c * chunk:(c + 1) * chunk])
        u = _dot(hn, wgu_ref[:, D_FF + c * chunk:D_FF + (c + 1) * chunk])
        act = (g * _sigmoid(g) * u).astype(BF16)
        acc = acc + _dot(act, wd_ref[c * chunk:(c + 1) * chunk, :])
    if final:
        acc = _rms(acc, gf_ref[...])
    o_ref[...] = acc


def _ffn(x1, attn, lw, ln_final, final):
    m = x1.shape[0]
    tm = min(256, m)
    chunk = 1408
    assert D_FF % chunk == 0

    def row(width):
        return pl.BlockSpec((tm, width), lambda i: (i, 0))

    return pl.pallas_call(
        functools.partial(_ffn_kernel, final=final, chunk=chunk),
        grid=(m // tm,),
        in_specs=[row(1024), row(256), _full((256, 1024)), _full((1, 1024)),
                  _full((1024, 2 * D_FF)), _full((D_FF, 1024)), _full((1, 1024))],
        out_specs=row(1024),
        out_shape=jax.ShapeDtypeStruct((m, 1024), F32),
        compiler_params=_params(1),
        name="ffn",
    )(x1, attn, lw["mem_wo"], lw["ln_ffn"], lw["w_gu"], lw["w_down"], ln_final)


def _memkv_kernel(x_ref, g_ref, w_ref, o_ref):
    o_ref[...] = _dot(_rms(x_ref[...], g_ref[...]).astype(BF16), w_ref[...])


def _memkv(mem, g, w):
    m = mem.shape[0]
    tm = min(256, m)
    return pl.pallas_call(
        _memkv_kernel,
        grid=(m // tm,),
        in_specs=[pl.BlockSpec((tm, 1024), lambda i: (i, 0)), _full((1, 1024)), _full((1024, 512))],
        out_specs=pl.BlockSpec((tm, 512), lambda i: (i, 0)),
        out_shape=jax.ShapeDtypeStruct((m, 512), F32),
        compiler_params=_params(1),
        name="mem_kv",
    )(mem, g, w)


def _page_specs(n_pages, block, layer, col_block=0):
    def spec(p):
        return pl.BlockSpec((1, 1) + block, lambda b, pt: (pt[b * n_pages + p], layer, 0, col_block))
    return [spec(p) for p in range(n_pages)]


def _softmax_with_new(s, s_new):
    m = jnp.maximum(jnp.max(s, axis=1, keepdims=True), s_new)
    p = jnp.exp(s - m)
    p_new = jnp.exp(s_new - m)
    denom = jnp.sum(p, axis=1, keepdims=True) + p_new
    return p, p_new, denom


def _diff_decode_kernel(pt_ref, q_ref, new_ref, slope_ref, lp_ref, sg_ref, *rest,
                        n_pages, lam_init):
    pages = rest[:n_pages]
    o_ref = rest[n_pages]
    b = pl.program_id(0)
    past = n_pages * PAGE_SIZE
    q = q_ref[0]
    kvs = [pg[0, 0].astype(BF16) for pg in pages]
    s = jnp.concatenate([_dot_nt(q, kv[:, :LANE]) for kv in kvs], axis=1)
    dist = (past - _iota((1, past), 1)).astype(F32)
    s = s - slope_ref[...] * dist
    new = new_ref[pl.ds(b % 8, 1), :]
    s_new = jnp.sum(q.astype(F32) * new[:, :LANE], axis=1, keepdims=True)
    p, p_new, denom = _softmax_with_new(s, s_new)
    pb = p.astype(BF16)
    acc = p_new * new[:, LANE:]
    for n, kv in enumerate(kvs):
        acc = acc + _dot(pb[:, n * PAGE_SIZE:(n + 1) * PAGE_SIZE], kv[:, LANE:])
    o = acc / denom
    lam = _diff_lambda(lp_ref, lam_init)
    for hh in range(DIFF_H):
        oh = o[2 * hh:2 * hh + 1] - lam * o[2 * hh + 1:2 * hh + 2]
        o_ref[0, hh:hh + 1, :] = _rms(oh, sg_ref[...]) * (1.0 - lam_init)


def _diff_decode(pt_flat, q3, new_rows, slopes, lp, sg, cache, layer, lam_init, n_pages):
    db = q3.shape[0]
    kern = functools.partial(_diff_decode_kernel, n_pages=n_pages, lam_init=lam_init)
    grid_spec = pltpu.PrefetchScalarGridSpec(
        num_scalar_prefetch=1,
        grid=(db,),
        in_specs=[pl.BlockSpec((1, 8, LANE), lambda b, pt: (b, 0, 0)),
                  pl.BlockSpec((8, 256), lambda b, pt: (b // 8, 0)),
                  pl.BlockSpec((8, 1), lambda b, pt: (0, 0)),
                  pl.BlockSpec((4, DIFF_HD), lambda b, pt: (0, 0)),
                  pl.BlockSpec((1, 2 * DIFF_HD), lambda b, pt: (0, 0))]
        + _page_specs(n_pages, (PAGE_SIZE, 256), layer),
        out_specs=pl.BlockSpec((1, DIFF_H, LANE), lambda b, pt: (b, 0, 0)),
    )
    return pl.pallas_call(
        kern,
        grid_spec=grid_spec,
        out_shape=jax.ShapeDtypeStruct((db, DIFF_H, LANE), F32),
        compiler_params=_params(1),
        name="diff_decode",
    )(pt_flat, q3, new_rows, slopes, lp, sg, *([cache] * n_pages))


def _mla_decode_kernel(pt_ref, q_ref, new_ref, wuv_ref, *rest, n_pages):
    pages = rest[:n_pages]
    o_ref = rest[n_pages]
    kvt_ref = rest[n_pages + 1]
    b = pl.program_id(0)
    scale = (MLA_NOPE + MLA_ROPE) ** -0.5
    width = MLA_KV_LORA + MLA_ROPE
    past = n_pages * PAGE_SIZE
    kvt_ref[width:, :] = jnp.zeros((256 - width, past), BF16)
    for n, pg in enumerate(pages):
        kvt_ref[:width, n * PAGE_SIZE:(n + 1) * PAGE_SIZE] = pg[0, 0].astype(BF16)
    q = q_ref[0]
    kvt = kvt_ref[...]
    s = _dot(q, kvt) * scale
    new = new_ref[pl.ds(b % 8, 1), :]
    s_new = jnp.sum(q.astype(F32) * new, axis=1, keepdims=True) * scale
    p, p_new, denom = _softmax_with_new(s, s_new)
    acc = p_new * new[:, :LANE] + _dot_nt(p.astype(BF16), kvt[:LANE, :])
    o_lat = acc / denom
    out = jnp.zeros((1, 512), F32)
    for hh in range(MLA_H):
        out = out + _dot(jnp.broadcast_to(o_lat[hh:hh + 1], (8, LANE)).astype(BF16),
                         wuv_ref[hh * LANE:(hh + 1) * LANE, :])[0:1]
    o_ref[0] = out


def _mla_decode(pt_flat, q3, new_rows, wuv_bd, cache_t, layer, n_pages):
    db = q3.shape[0]
    width = MLA_KV_LORA + MLA_ROPE
    grid_spec = pltpu.PrefetchScalarGridSpec(
        num_scalar_prefetch=1,
        grid=(db,),
        in_specs=[pl.BlockSpec((1, 8, 256), lambda b, pt: (b, 0, 0)),
                  pl.BlockSpec((8, 256), lambda b, pt: (b // 8, 0)),
                  pl.BlockSpec((1024, 512), lambda b, pt: (0, 0))]
        + _page_specs(n_pages, (width, PAGE_SIZE), layer),
        out_specs=pl.BlockSpec((1, 1, 512), lambda b, pt: (b, 0, 0)),
        scratch_shapes=[pltpu.VMEM((256, n_pages * PAGE_SIZE), BF16)],
    )
    return pl.pallas_call(
        functools.partial(_mla_decode_kernel, n_pages=n_pages),
        grid_spec=grid_spec,
        out_shape=jax.ShapeDtypeStruct((db, 1, 512), F32),
        compiler_params=_params(1),
        name="mla_decode",
    )(pt_flat, q3, new_rows, wuv_bd, *([cache_t] * n_pages))


def _nsa_decode_cmp_kernel(pt_ref, q_ref, wbig_ref, pe_ref, b1_ref, w2_ref, c2s_ref, *rest,
                           n_pages, ns, n_sel):
    pages = rest[:n_pages]
    oc_ref, sel_ref = rest[n_pages], rest[n_pages + 1]
    seg_per_page = PAGE_SIZE // CMP_STRIDE
    n_seg = n_pages * seg_per_page
    past = n_pages * PAGE_SIZE

    def load_rows(r):
        return jnp.concatenate([pg[0, 0, pl.ds(r, seg_per_page, stride=CMP_STRIDE), :] for pg in pages],
                               axis=0)

    kvcmp = _compress(load_rows, n_seg, wbig_ref, pe_ref, b1_ref, w2_ref).astype(BF16)
    q = q_ref[0].astype(F32)
    qpos = jnp.full((8, 1), past, jnp.int32)
    cmp_end = _iota((1, n_seg), 1) * CMP_STRIDE + (CMP_BLOCK - 1)
    dist_i = qpos - cmp_end
    ok = dist_i >= 0
    dist = dist_i.astype(F32)
    p_sum = jnp.zeros((8, n_seg), F32)
    for hh in range(NSA_H):
        qh = jnp.broadcast_to(q[:, hh * LANE:(hh + 1) * LANE], (8, LANE)).astype(BF16)
        s = jnp.where(ok, _dot_nt(qh, kvcmp) - _alibi(NSA_H, hh) * dist, NEG)
        e = jnp.exp(s - jnp.max(s, axis=1, keepdims=True))
        p = jnp.where(ok, e / jnp.sum(e, axis=1, keepdims=True), 0.0)
        oc_ref[0, :, hh * LANE:(hh + 1) * LANE] = _dot(p.astype(BF16), kvcmp)[0:1]
        p_sum = p_sum + p
    p_hi = p_sum.astype(BF16)
    p_lo = (p_sum - p_hi.astype(F32)).astype(BF16)
    c2s = c2s_ref[...]
    imp = _dot(p_hi, c2s) + _dot(p_lo, c2s)
    _, idxs = _select_blocks(imp[0:1], qpos[0:1], ns, n_sel, 1)
    lane = _iota((1, LANE), 1)
    sel = jnp.zeros((1, LANE), F32)
    for k, idx in enumerate(idxs):
        sel = jnp.where(lane == k, idx, sel)
    sel_ref[0] = sel.astype(jnp.int32)


def _nsa_decode_cmp(pt_flat, nq3, lw, c2s, cache, layer, n_pages):
    db = nq3.shape[0]
    ns = (n_pages * PAGE_SIZE + 1 + SEL_BLOCK - 1) // SEL_BLOCK
    n_sel = min(N_SEL, ns)
    ncp, nsp = c2s.shape
    grid_spec = pltpu.PrefetchScalarGridSpec(
        num_scalar_prefetch=1,
        grid=(db,),
        in_specs=[pl.BlockSpec((1, 1, 1024), lambda b, pt: (b, 0, 0)),
                  pl.BlockSpec((CMP_STRIDE * LANE, 512), lambda b, pt: (0, 0)),
                  pl.BlockSpec((16, CMP_STRIDE * LANE), lambda b, pt: (0, 0)),
                  pl.BlockSpec((1, 256), lambda b, pt: (0, 0)),
                  pl.BlockSpec((256, LANE), lambda b, pt: (0, 0)),
                  pl.BlockSpec((ncp, nsp), lambda b, pt: (0, 0))]
        + _page_specs(n_pages, (PAGE_SIZE, LANE), layer),
        out_specs=[pl.BlockSpec((1, 1, 1024), lambda b, pt: (b, 0, 0)),
                   pl.BlockSpec((1, 1, LANE), lambda b, pt: (b, 0, 0))],
    )
    return pl.pallas_call(
        functools.partial(_nsa_decode_cmp_kernel, n_pages=n_pages, ns=ns, n_sel=n_sel),
        grid_spec=grid_spec,
        out_shape=[jax.ShapeDtypeStruct((db, 1, 1024), F32),
                   jax.ShapeDtypeStruct((db, 1, LANE), jnp.int32)],
        compiler_params=_params(1),
        name="nsa_decode_cmp",
    )(pt_flat, nq3, lw["cmp_wbig"], lw["cmp_pe"], lw["cmp_b1"], lw["cmp_w2"], c2s,
      *([cache] * n_pages))


def _nsa_decode_sel_kernel(pg_ref, hf_ref, ix_ref, q_ref, newkv_ref, neww_ref, gate_ref, slope_ref,
                           oc_ref, win_ref, *rest, n_sel, past):
    blocks = rest[:n_sel]
    o_ref, nwin_ref = rest[n_sel], rest[n_sel + 1]
    kv_ref = rest[n_sel + 2]
    b = pl.program_id(0)
    row = pl.ds(b % 8, 1)
    q = q_ref[0].astype(F32)
    qf = jnp.concatenate([q[:, hh * LANE:(hh + 1) * LANE] for hh in range(NSA_H)], axis=0)
    qs = qf.astype(BF16)
    slope = slope_ref[...]
    gates = gate_ref[row, :]

    cur_blk = past // SEL_BLOCK
    pos_rows = []
    n_new = jnp.int32(0)
    for k in range(n_sel):
        kv_ref[k * SEL_BLOCK:(k + 1) * SEL_BLOCK, :] = blocks[k][0, 0].astype(BF16)
        idx = ix_ref[b * n_sel + k]
        idx_pos = jnp.where(idx < cur_blk, idx, cur_blk + 1)
        pos_rows.append(idx_pos * SEL_BLOCK + _iota((1, SEL_BLOCK), 1))
        n_new = n_new + (idx == cur_blk).astype(jnp.int32)
    kv = kv_ref[...]
    dist_i = past - jnp.concatenate(pos_rows, axis=1)
    ok = dist_i >= 0
    s = jnp.where(ok, _dot_nt(qs, kv) - slope * dist_i.astype(F32), NEG)
    newkv = newkv_ref[row, :]
    s_new = jnp.sum(qf * newkv, axis=1, keepdims=True) + jnp.where(n_new > 0, 0.0, NEG)
    p, p_new, denom = _softmax_with_new(s, s_new)
    o_s = (p_new * newkv + _dot(p.astype(BF16), kv)) / denom

    win = win_ref[0, 0]
    n_win = win.shape[0]
    winb = win.astype(BF16)
    dist_w = (n_win - _iota((1, n_win), 1)).astype(F32)
    ok_w = dist_w <= float(WINDOW)
    s_w = jnp.where(ok_w, _dot_nt(qs, winb) - slope * dist_w, NEG)
    neww = neww_ref[row, :]
    s_wn = jnp.sum(qf * neww, axis=1, keepdims=True)
    pw, pw_new, denom_w = _softmax_with_new(s_w, s_wn)
    o_w = (pw_new * neww + _dot(pw.astype(BF16), winb)) / denom_w

    lane = _iota((8, LANE), 1)
    head3 = _iota((8, LANE), 0) * 3

    def gate(g):
        return jnp.sum(jnp.where(lane == head3 + g, gates, 0.0), axis=1, keepdims=True)

    o_c = jnp.concatenate([oc_ref[0, :, hh * LANE:(hh + 1) * LANE] for hh in range(NSA_H)], axis=0)
    o = gate(0) * o_c + gate(1) * o_s + gate(2) * o_w
    for hh in range(NSA_H):
        o_ref[0, :, hh * LANE:(hh + 1) * LANE] = o[hh:hh + 1]

    rolled = pltpu.roll(win, n_win - 1, 0)
    nwin_ref[0, 0] = jnp.where(_iota((n_win, LANE), 0) == n_win - 1, neww, rolled)


def _nsa_decode_sel(sel_pg, sel_hf, sel_ix, nq3, slc_new, win_new, gates, slopes, o_c, state_win, cache,
                    layer, past):
    db = nq3.shape[0]
    n_sel = sel_ix.shape[0] // db
    n_win = state_win.shape[2]

    def blk_spec(k):
        return pl.BlockSpec((1, 1, SEL_BLOCK, LANE),
                            lambda b, pg, hf, ix: (pg[b * n_sel + k], layer, hf[b * n_sel + k], 1))

    def c3(b, pg, hf, ix):
        return (b, 0, 0)

    def g8(b, pg, hf, ix):
        return (b // 8, 0)

    grid_spec = pltpu.PrefetchScalarGridSpec(
        num_scalar_prefetch=3,
        grid=(db,),
        in_specs=[pl.BlockSpec((1, 1, 1024), c3),
                  pl.BlockSpec((8, LANE), g8), pl.BlockSpec((8, LANE), g8), pl.BlockSpec((8, LANE), g8),
                  pl.BlockSpec((8, 1), lambda b, pg, hf, ix: (0, 0)),
                  pl.BlockSpec((1, 1, 1024), c3),
                  pl.BlockSpec((1, 1, n_win, LANE), lambda b, pg, hf, ix: (b, layer, 0, 0))]
        + [blk_spec(k) for k in range(n_sel)],
        out_specs=[pl.BlockSpec((1, 1, 1024), c3),
                   pl.BlockSpec((1, 1, n_win, LANE), lambda b, pg, hf, ix: (b, 0, 0, 0))],
        scratch_shapes=[pltpu.VMEM((n_sel * SEL_BLOCK, LANE), BF16)],
    )
    return pl.pallas_call(
        functools.partial(_nsa_decode_sel_kernel, n_sel=n_sel, past=past),
        grid_spec=grid_spec,
        out_shape=[jax.ShapeDtypeStruct((db, 1, 1024), F32),
                   jax.ShapeDtypeStruct((db, 1, n_win, LANE), F32)],
        compiler_params=_params(1),
        name="nsa_decode_sel",
    )(sel_pg, sel_hf, sel_ix, nq3, slc_new, win_new, gates, slopes, o_c, state_win, *([cache] * n_sel))


def _pad_cols(w, width):
    return jnp.pad(w, ((0, 0), (0, width - w.shape[1])))


def _block_diag(blocks):
    n, r, c = blocks.shape
    eye = jnp.eye(n, dtype=blocks.dtype)
    return (blocks[:, :, None, :] * eye[:, None, :, None]).reshape(n * r, n * c)


def _layer_weights(l, p):
    w = p["w_in"][l]
    cuts = np.cumsum([0, 512, 256, 256, 128, 32, 512, 256, 128, 24, 3072])
    d_q, d_kv, m_cq, m_ckv, m_kr, n_q, n_kv, n_win, n_g, br_g = [w[:, cuts[k]:cuts[k + 1]] for k in range(10)]
    eye2 = jnp.eye(2, dtype=F32)
    d_q = d_q * (DIFF_HD ** -0.5)
    n_q = n_q * (NSA_HD ** -0.5)
    dq_bd = (d_q.reshape(D_MODEL, DIFF_H, 2, 1, DIFF_HD) * eye2[None, None, :, :, None]).reshape(D_MODEL, 1024)
    half = MLA_ROPE // 2
    m_kr_sw = jnp.concatenate([m_kr[:, half:], m_kr[:, :half]], axis=1)
    nq_pad = jnp.pad(n_q.reshape(D_MODEL, NSA_H, NSA_HD), ((0, 0), (0, 0), (0, LANE - NSA_HD))).reshape(D_MODEL, 1024)
    w_all = jnp.concatenate([dq_bd, d_kv, m_cq, m_ckv, _pad_cols(m_kr, LANE), _pad_cols(m_kr_sw, LANE),
                             nq_pad, n_kv, n_win, _pad_cols(n_g, LANE), br_g], axis=1).astype(BF16)
    assert w_all.shape[1] == N_PROJ

    wuq = p["mla_w_uq"][l].reshape(MLA_Q_LORA, MLA_H, MLA_NOPE + MLA_ROPE)
    wn = wuq[:, :, :MLA_NOPE].reshape(MLA_Q_LORA, MLA_H * MLA_NOPE)
    wr = wuq[:, :, MLA_NOPE:]
    wr_sw = jnp.concatenate([wr[:, :, half:], wr[:, :, :half]], axis=2)
    pad3 = ((0, 0), (0, 0), (0, LANE - MLA_ROPE))
    wr_p = jnp.pad(wr, pad3).reshape(MLA_Q_LORA, MLA_H * LANE)
    wrs_p = jnp.pad(wr_sw, pad3).reshape(MLA_Q_LORA, MLA_H * LANE)
    wuk_bd = _block_diag(jnp.transpose(p["mla_w_uk"][l], (1, 2, 0)))
    wuv_bd = _block_diag(jnp.transpose(p["mla_w_uv"][l], (1, 0, 2)))

    w1 = p["nsa_cmp_w1"][l].reshape(2, 2, CMP_STRIDE, NSA_HD, CMP_HIDDEN)
    zero = jnp.zeros((CMP_STRIDE, NSA_HD, CMP_HIDDEN), F32)
    top = jnp.concatenate([w1[0, 0], zero, w1[0, 1], zero], axis=2)
    bot = jnp.concatenate([zero, w1[1, 0], zero, w1[1, 1]], axis=2)
    cmp_wbig = jnp.concatenate([top, bot], axis=1).reshape(CMP_STRIDE * LANE, 512).astype(BF16)
    pe = p["nsa_cmp_pe"][l]
    pe_rows = jnp.concatenate([pe[0], pe[1]], axis=1)
    cmp_pe = jnp.zeros((16, CMP_STRIDE * LANE), F32)
    cmp_pe = cmp_pe.at[0].set(pe_rows[:CMP_STRIDE].reshape(-1)).at[1].set(pe_rows[CMP_STRIDE:].reshape(-1))
    cmp_b1 = p["nsa_cmp_b1"][l].reshape(1, 2 * CMP_HIDDEN)
    cmp_w2 = _block_diag(p["nsa_cmp_w2"][l]).astype(BF16)

    wb = p["w_branch"][l]
    wb2 = jnp.pad(wb[2].reshape(NSA_H, NSA_HD, D_MODEL), ((0, 0), (LANE - NSA_HD, 0), (0, 0))).reshape(1024, D_MODEL)
    wq4 = p["mem_wq"][l].reshape(D_MODEL, MEM_H, MEM_HD) * (MEM_HD ** -0.5)
    eye4 = jnp.eye(MEM_H, dtype=F32)
    mem_wq_bd = (wq4[:, :, None, :] * eye4[None, :, :, None]).reshape(D_MODEL, 1024)
    return dict(
        ln_mix=p["ln_mix"][l][None], w_all=w_all, gq=p["mla_q_norm"][l][None], gkv=p["mla_kv_norm"][l][None],
        wn=wn.astype(BF16), wuk_bd=wuk_bd.astype(BF16), wr=wr_p.astype(BF16), wrs=wrs_p.astype(BF16),
        wuv_bd=wuv_bd.astype(BF16), lp=p["diff_lambda"][l], sg=p["diff_subln"][l][None],
        cmp_wbig=cmp_wbig, cmp_pe=cmp_pe, cmp_b1=cmp_b1, cmp_w2=cmp_w2,
        wb0=wb[0].astype(BF16), wb1=wb[1].astype(BF16), wb2=wb2.astype(BF16),
        w_out=p["w_out"][l].astype(BF16), ln_xattn=p["ln_xattn"][l][None], mem_wq_bd=mem_wq_bd.astype(BF16),
        mem_wo=p["mem_wo"][l].astype(BF16), ln_ffn=p["ln_ffn"][l][None],
        w_gu=p["ffn_w_gu"][l].astype(BF16), w_down=p["ffn_w_down"][l].astype(BF16),
        ln_mem=p["ln_mem"][l][None], mem_wkv=p["mem_wkv"][l].astype(BF16),
    )


def _rope_tables(pos):
    half = MLA_ROPE // 2
    inv = ROPE_THETA ** (-np.arange(half, dtype=np.float32) / half)
    ang = pos.astype(F32)[:, None] * jnp.asarray(inv, F32)
    cos, sin = jnp.cos(ang), jnp.sin(ang)
    zeros = jnp.zeros((pos.shape[0], LANE - MLA_ROPE), F32)
    return jnp.concatenate([cos, cos, zeros], axis=1), jnp.concatenate([-sin, sin, zeros], axis=1)


def _cmp_to_sel(ncp, nsp):
    c0 = np.arange(ncp)[:, None] * CMP_STRIDE
    s0 = np.arange(nsp)[None, :] * SEL_BLOCK
    ov = np.clip(np.minimum(c0 + CMP_BLOCK, s0 + SEL_BLOCK) - np.maximum(c0, s0), 0, None)
    return ov / CMP_BLOCK


def _round_up(n, m):
    return -(-n // m) * m


def kernel(x_prompt, x_sample, mem_prompt, cache_diff, cache_mla, cache_nsa, state_nsa_win, cache_mem, page_table, ln_mix, w_in, diff_lambda, diff_subln, mla_q_norm, mla_kv_norm, mla_w_uq, mla_w_uk, mla_w_uv, nsa_cmp_pe, nsa_cmp_w1, nsa_cmp_b1, nsa_cmp_w2, w_branch, w_out, ln_xattn, ln_mem, mem_wq, mem_wkv, mem_wo, ln_ffn, ffn_w_gu, ffn_w_down, ln_final):
    p = dict(ln_mix=ln_mix, w_in=w_in, diff_lambda=diff_lambda, diff_subln=diff_subln,
             mla_q_norm=mla_q_norm, mla_kv_norm=mla_kv_norm, mla_w_uq=mla_w_uq, mla_w_uk=mla_w_uk,
             mla_w_uv=mla_w_uv, nsa_cmp_pe=nsa_cmp_pe, nsa_cmp_w1=nsa_cmp_w1, nsa_cmp_b1=nsa_cmp_b1,
             nsa_cmp_w2=nsa_cmp_w2, w_branch=w_branch, w_out=w_out, ln_xattn=ln_xattn, ln_mem=ln_mem,
             mem_wq=mem_wq, mem_wkv=mem_wkv, mem_wo=mem_wo, ln_ffn=ln_ffn, ffn_w_gu=ffn_w_gu,
             ffn_w_down=ffn_w_down)
    bsz, seq, _ = x_prompt.shape
    db = x_sample.shape[0]
    depth = ln_mix.shape[0]
    n_pages = page_table.shape[1]
    past = n_pages * PAGE_SIZE
    n_mem = mem_prompt.shape[1]
    t = min(256, seq)
    assert seq % t == 0 and seq % SEL_BLOCK == 0 and x_sample.shape[1] == 1 and db % 8 == 0

    cos_p, sin_p = _rope_tables(jnp.arange(seq, dtype=jnp.int32))
    cos_s, sin_s = _rope_tables(jnp.full((min(256, db),), past, jnp.int32))
    c2st_p = jnp.asarray(_cmp_to_sel(seq // CMP_STRIDE, _round_up(seq // SEL_BLOCK, LANE)).T, BF16)
    ns_s = (past + 1 + SEL_BLOCK - 1) // SEL_BLOCK
    c2s_s = jnp.asarray(_cmp_to_sel(past // CMP_STRIDE, _round_up(ns_s, LANE)), BF16)
    pt_flat = page_table.reshape(-1)
    slopes_diff = jnp.asarray(np.repeat([_alibi(DIFF_H, h) for h in range(DIFF_H)], 2)[:, None], F32)
    slopes_nsa = jnp.asarray(np.array([_alibi(NSA_H, h) for h in range(NSA_H)])[:, None], F32)
    cache_mla_t = jnp.swapaxes(cache_mla, 2, 3)

    xp = x_prompt.reshape(bsz * seq, D_MODEL)
    xs = x_sample.reshape(db, D_MODEL)
    mem_flat = mem_prompt.reshape(bsz * n_mem, D_MODEL)
    outs = {k: [] for k in ("dp", "ds", "mp", "ms", "np", "ns", "wp", "ws", "memp")}
    for l in range(depth):
        lw = _layer_weights(l, p)
        lam_init = 0.8 - 0.6 * math.exp(-0.3 * l)
        final = l == depth - 1
        ln_f = ln_final[None]

        mem_kv = _memkv(mem_flat, lw["ln_mem"], lw["mem_wkv"]).reshape(bsz, n_mem, 512)
        (dq, dkv, dkvb, qmla, mrow, kvmla, nq, nkv, slcb, nwin, nwinb, gates, brg,
         dvt, ckvt, slct, wint, gatet) = _proj_in(xp, lw, cos_p, sin_p, seq // t, True)

        def b3(a):
            return a.reshape(bsz, seq, a.shape[-1])

        o_diff = _diff_prefill(b3(dq), b3(dkvb), dvt, lw["lp"], lw["sg"], lam_init, t)
        o_mla = _mla_prefill(b3(qmla), b3(kvmla), ckvt, lw["wuv_bd"], t)
        kvcmp, kvcmpt = _compress_prefill(b3(nkv), lw)
        o_nsa = _nsa_prefill(b3(nq), kvcmp, kvcmpt, b3(slcb), slct, b3(nwinb), wint, gatet, c2st_p, t)
        x1, hq = _merge(xp, o_diff.reshape(-1, 512), o_mla.reshape(-1, 512), o_nsa.reshape(-1, 1024), brg, lw)
        attn = _xattn_prefill(b3(hq), mem_kv, t).reshape(-1, 256)
        xp = _ffn(x1, attn, lw, ln_f, final)
        outs["dp"].append(b3(dkv)); outs["mp"].append(b3(mrow)); outs["np"].append(b3(nkv))
        outs["wp"].append(b3(nwin)[:, seq - min(WINDOW, seq):]); outs["memp"].append(mem_kv)

        (dq, dkv, dkvb, qmla, mrow, kvmla, nq, nkv, slcb, nwin, nwinb, gates, brg) = _proj_in(
            xs, lw, cos_s, sin_s, 1, False)
        o_diff = _diff_decode(pt_flat, dq.reshape(db, 8, LANE), dkv, slopes_diff, lw["lp"], lw["sg"],
                              cache_diff, l, lam_init, n_pages).reshape(db, 512)
        o_mla = _mla_decode(pt_flat, qmla.reshape(db, 8, 256), kvmla.astype(F32), lw["wuv_bd"], cache_mla_t, l,
                            n_pages).reshape(db, 512)
        nq3 = nq.reshape(db, 1, 1024)
        o_c, sel = _nsa_decode_cmp(pt_flat, nq3, lw, c2s_s, cache_nsa, l, n_pages)
        n_sel = min(N_SEL, ns_s)
        sel_ix = sel[:, 0, :n_sel]
        cached_ix = jnp.minimum(sel_ix, past // SEL_BLOCK - 1)
        blocks_per_page = PAGE_SIZE // SEL_BLOCK
        sel_pg = jnp.take_along_axis(page_table, cached_ix // blocks_per_page, axis=1)
        sel_hf = cached_ix % blocks_per_page
        o_nsa, new_win = _nsa_decode_sel(sel_pg.reshape(-1), sel_hf.reshape(-1), sel_ix.reshape(-1), nq3,
                                         nkv[:, LANE:], nwin, gates, slopes_nsa, o_c, state_nsa_win,
                                         cache_nsa, l, past)
        x1, hq = _merge(xs, o_diff, o_mla, o_nsa.reshape(db, 1024), brg, lw)
        attn = _xattn_decode(hq, cache_mem, l)
        xs = _ffn(x1, attn, lw, ln_f, final)
        outs["ds"].append(dkv[:, None]); outs["ms"].append(mrow[:, None]); outs["ns"].append(nkv[:, None])
        outs["ws"].append(new_win[:, 0])

    y_prompt = xp.reshape(bsz, seq, D_MODEL)
    y_sample = xs.reshape(db, 1, D_MODEL)
    st = lambda k: jnp.stack(outs[k], axis=1)
    return (y_prompt, y_sample, st("dp"), st("ds"), st("mp"), st("ms"), st("np"), st("ns"),
            st("wp"), st("ws"), st("memp"))
```

```python
import functools
import math

import numpy as np
import jax
import jax.numpy as jnp
from jax import lax
from jax.experimental import pallas as pl
from jax.experimental.pallas import tpu as pltpu

F32 = jnp.float32
BF16 = jnp.bfloat16

D_MODEL = 1024
PAGE_SIZE = 128
DIFF_HD = 64
DIFF_H = 4
MLA_H = 8
MLA_NOPE = 64
MLA_ROPE = 32
MLA_V = 64
MLA_Q_LORA = 256
MLA_KV_LORA = 128
ROPE_THETA = 10000.0
NSA_H = 8
NSA_HD = 64
CMP_BLOCK = 32
CMP_STRIDE = 16
CMP_HIDDEN = 128
SEL_BLOCK = 64
SEL_SHIFT = 6
N_SEL = 16
WINDOW = 512
MEM_H = 4
MEM_HD = 64
D_FF = 2816
N_BRANCH = 3
NEG = -1e30
FORCE = 1e6
EPS = 1e-6
LANE = 128
VMEM_LIMIT = 56 * 1024 * 1024

OFF_DQ = 0
OFF_DKV = OFF_DQ + 1024
OFF_MCQ = OFF_DKV + 256
OFF_MCKV = OFF_MCQ + 256
OFF_MKR = OFF_MCKV + 128
OFF_MKRS = OFF_MKR + 128
OFF_NQ = OFF_MKRS + 128
OFF_NKV = OFF_NQ + 1024
OFF_NWIN = OFF_NKV + 256
OFF_NG = OFF_NWIN + 128
OFF_BRG = OFF_NG + 128
N_PROJ = OFF_BRG + N_BRANCH * D_MODEL


def _dot(a, b):
    return jnp.dot(a, b, preferred_element_type=F32)


def _dot_nt(a, b):
    return lax.dot_general(a, b, (((1,), (1,)), ((), ())), preferred_element_type=F32)


def _rms(x, g):
    return x * lax.rsqrt(jnp.mean(x * x, axis=-1, keepdims=True) + EPS) * g


def _sigmoid(x):
    return 1.0 / (1.0 + jnp.exp(-x))


def _iota(shape, dim):
    return lax.broadcasted_iota(jnp.int32, shape, dim)


def _params(n_grid):
    return pltpu.CompilerParams(dimension_semantics=("arbitrary",) * n_grid,
                                vmem_limit_bytes=VMEM_LIMIT)


def _full(shape):
    n = len(shape)
    return pl.BlockSpec(shape, lambda *a: (0,) * n)


def _alibi(n, h):
    return float(2.0 ** (-8.0 * (h + 1) / n))


def _proj_in_kernel(x_ref, g_ref, w_ref, gq_ref, gkv_ref, wn_ref, wuk_ref, wr_ref, wrs_ref,
                    cos_ref, sin_ref,
                    dq_ref, dkv_ref, dkvb_ref, qmla_ref, mrow_ref, kvmla_ref, nq_ref,
                    nkv_ref, slcb_ref, nwin_ref, nwinb_ref, gate_ref, brg_ref, *t_refs):
    x = x_ref[...]
    h = _rms(x, g_ref[...]).astype(BF16)

    def piece(off, width):
        return _dot(h, w_ref[:, off:off + width])

    dq_ref[...] = piece(OFF_DQ, 1024).astype(BF16)
    dkv = piece(OFF_DKV, 256)
    dkv_ref[...] = dkv
    dkvb_ref[...] = dkv.astype(BF16)

    cos = cos_ref[...]
    sin = sin_ref[...]
    cqn = _rms(piece(OFF_MCQ, 256), gq_ref[...]).astype(BF16)
    q_nope = _dot(cqn, wn_ref[...]).astype(BF16)
    q_lat = _dot(q_nope, wuk_ref[...])
    q_r = _dot(cqn, wr_ref[...])
    q_rs = _dot(cqn, wrs_ref[...])
    for hh in range(MLA_H):
        sl = slice(hh * LANE, (hh + 1) * LANE)
        qmla_ref[:, hh * 256:hh * 256 + LANE] = q_lat[:, sl].astype(BF16)
        qmla_ref[:, hh * 256 + LANE:(hh + 1) * 256] = (q_r[:, sl] * cos + q_rs[:, sl] * sin).astype(BF16)
    ckv = _rms(piece(OFF_MCKV, 128), gkv_ref[...])
    kr2 = piece(OFF_MKR, 256)
    kr = kr2[:, :LANE] * cos + kr2[:, LANE:] * sin
    mrow_ref[:, :MLA_KV_LORA] = ckv
    mrow_ref[:, MLA_KV_LORA:] = kr[:, :MLA_ROPE]
    kvmla_ref[:, :LANE] = ckv.astype(BF16)
    kvmla_ref[:, LANE:] = kr.astype(BF16)
    nq_ref[...] = piece(OFF_NQ, 1024).astype(BF16)
    nkv = piece(OFF_NKV, 256)
    nkv_ref[...] = nkv
    slcb_ref[...] = nkv[:, LANE:].astype(BF16)
    nwin = piece(OFF_NWIN, 128)
    nwin_ref[...] = nwin
    nwinb_ref[...] = nwin.astype(BF16)
    gates = _sigmoid(piece(OFF_NG, 128))
    gate_ref[...] = gates
    for c in range(3):
        brg_ref[:, c * 1024:(c + 1) * 1024] = _sigmoid(piece(OFF_BRG + c * 1024, 1024))
    if t_refs:
        dvt_ref, ckvt_ref, slct_ref, wint_ref, gatet_ref = t_refs
        dvt_ref[0] = dkv[:, LANE:].T.astype(BF16)
        ckvt_ref[0] = ckv.T.astype(BF16)
        slct_ref[0] = nkv[:, LANE:].T.astype(BF16)
        wint_ref[0] = nwin.T.astype(BF16)
        gatet_ref[0] = gates.T


def _proj_in(x, lw, cos_t, sin_t, n_pos_blocks, transposed):
    m = x.shape[0]
    tm = min(256, m)
    grid = (m // tm,)

    def row(width):
        return pl.BlockSpec((tm, width), lambda i: (i, 0))

    pos_spec = pl.BlockSpec((tm, LANE), lambda i: (i % n_pos_blocks, 0))
    in_specs = [row(D_MODEL), _full((1, D_MODEL)), _full((D_MODEL, N_PROJ)),
                _full((1, MLA_Q_LORA)), _full((1, MLA_KV_LORA)),
                _full((MLA_Q_LORA, 512)), _full((512, 1024)),
                _full((MLA_Q_LORA, 1024)), _full((MLA_Q_LORA, 1024)),
                pos_spec, pos_spec]
    outs = [(1024, BF16), (256, F32), (256, BF16), (2048, BF16), (160, F32), (256, BF16),
            (1024, BF16), (256, F32), (128, BF16), (128, F32), (128, BF16), (128, F32),
            (3072, F32)]
    out_specs = [row(w) for w, _ in outs]
    out_shape = [jax.ShapeDtypeStruct((m, w), dt) for w, dt in outs]
    if transposed:
        for dt in (BF16, BF16, BF16, BF16, F32):
            out_specs.append(pl.BlockSpec((1, LANE, tm), lambda i: (i, 0, 0)))
            out_shape.append(jax.ShapeDtypeStruct((m // tm, LANE, tm), dt))
    return pl.pallas_call(
        _proj_in_kernel,
        grid=grid,
        in_specs=in_specs,
        out_specs=out_specs,
        out_shape=out_shape,
        compiler_params=_params(1),
        name="proj_in",
    )(x, lw["ln_mix"], lw["w_all"], lw["gq"], lw["gkv"], lw["wn"], lw["wuk_bd"],
      lw["wr"], lw["wrs"], cos_t, sin_t)


def _init_state(m_ref, l_ref, acc_ref):
    m_ref[...] = jnp.full(m_ref.shape, NEG, F32)
    l_ref[...] = jnp.zeros(l_ref.shape, F32)
    acc_ref[...] = jnp.zeros(acc_ref.shape, F32)


def _flash_tile(n, score_fn, shift_fn, vt, state):
    m_ref, l_ref, acc_ref, s_ref, p_ref, a_ref = state

    def scores(sl):
        st = score_fn(sl)
        s_ref[sl] = st
        a_ref[sl] = jnp.max(st, axis=0, keepdims=True)

    def softmax(sl):
        shift = shift_fn(sl)
        m_prev = m_ref[sl]
        m_new = jnp.maximum(m_prev, a_ref[sl] + shift)
        alpha = jnp.exp(m_prev - m_new)
        p = jnp.exp(s_ref[sl] - (m_new - shift))
        l_ref[sl] = alpha * l_ref[sl] + jnp.sum(p, axis=0, keepdims=True)
        m_ref[sl] = m_new
        a_ref[sl] = alpha
        p_ref[sl] = p.astype(BF16)

    def values(sl):
        acc_ref[sl] = a_ref[sl] * acc_ref[sl] + _dot(vt, p_ref[sl])

    for sweep in (scores, softmax, values):
        for sl in range(n):
            sweep(sl)


def _rel_dist_t(q0, k0, t):
    return (q0 - k0) + _iota((t, t), 1) - _iota((t, t), 0)


def _state_scratch(t):
    return [pltpu.VMEM((8, 1, t), F32), pltpu.VMEM((8, 1, t), F32), pltpu.VMEM((8, LANE, t), F32),
            pltpu.VMEM((8, t, t), F32), pltpu.VMEM((8, t, t), BF16), pltpu.VMEM((8, 1, t), F32)]


def _diff_lambda(lp_ref, lam_init):
    lp = lp_ref[...]
    a = jnp.sum(lp[0:1] * lp[1:2], axis=(0, 1), keepdims=True)
    b = jnp.sum(lp[2:3] * lp[3:4], axis=(0, 1), keepdims=True)
    return jnp.exp(a) - jnp.exp(b) + lam_init


def _diff_prefill_kernel(q_ref, kv_ref, vt_ref, lp_ref, sg_ref, o_ref, *state, t, lam_init):
    m_ref, l_ref, acc_ref = state[:3]
    i = pl.program_id(1)
    _init_state(m_ref, l_ref, acc_ref)
    key_in_tile = _iota((t, t), 0).astype(F32)
    biases = [_alibi(DIFF_H, hh) * key_in_tile for hh in range(DIFF_H)]

    def tile(j, masked):
        k0 = pl.multiple_of(j * t, t)
        k12 = kv_ref[0, pl.ds(k0, t), :LANE]
        tile_off = ((j - i) * t).astype(F32)
        if masked:
            ok = _rel_dist_t(0, 0, t) >= 0

        def score(sl):
            st = _dot_nt(k12, q_ref[0, :, sl * LANE:(sl + 1) * LANE]) + biases[sl // 2]
            return jnp.where(ok, st, NEG) if masked else st

        _flash_tile(2 * DIFF_H, score, lambda sl: _alibi(DIFF_H, sl // 2) * tile_off, vt_ref[j], state)

    def body(j, carry):
        tile(j, False)
        return carry

    lax.fori_loop(0, i, body, 0)
    tile(i, True)

    lam = _diff_lambda(lp_ref, lam_init)
    for hh in range(DIFF_H):
        o1 = acc_ref[2 * hh] / l_ref[2 * hh]
        o2 = acc_ref[2 * hh + 1] / l_ref[2 * hh + 1]
        o = (o1 - lam * o2).T
        o_ref[0, :, hh * LANE:(hh + 1) * LANE] = (_rms(o, sg_ref[...]) * (1.0 - lam_init)).astype(BF16)


def _diff_prefill(dq, dkvb, dvt, lp, sg, lam_init, t):
    b, s, _ = dq.shape
    nt = s // t
    kern = functools.partial(_diff_prefill_kernel, t=t, lam_init=lam_init)
    return pl.pallas_call(
        kern,
        grid=(b, nt),
        in_specs=[pl.BlockSpec((1, t, 1024), lambda bb, i: (bb, i, 0)),
                  pl.BlockSpec((1, s, 256), lambda bb, i: (bb, 0, 0)),
                  pl.BlockSpec((nt, LANE, t), lambda bb, i: (bb, 0, 0)),
                  _full((4, DIFF_HD)), _full((1, 2 * DIFF_HD))],
        out_specs=pl.BlockSpec((1, t, 512), lambda bb, i: (bb, i, 0)),
        out_shape=jax.ShapeDtypeStruct((b, s, 512), BF16),
        scratch_shapes=_state_scratch(t),
        compiler_params=_params(2),
        name="diff_prefill",
    )(dq, dkvb, dvt, lp, sg)


def _mla_prefill_kernel(q_ref, kv_ref, vt_ref, wuv_ref, o_ref, *state, t):
    m_ref, l_ref, acc_ref = state[:3]
    i = pl.program_id(1)
    scale = (MLA_NOPE + MLA_ROPE) ** -0.5
    _init_state(m_ref, l_ref, acc_ref)

    def tile(j, masked):
        k0 = pl.multiple_of(j * t, t)
        kv = kv_ref[0, pl.ds(k0, t), :]
        if masked:
            ok = _rel_dist_t(0, 0, t) >= 0

        def score(hh):
            st = _dot_nt(kv, q_ref[0, :, hh * 256:(hh + 1) * 256]) * scale
            return jnp.where(ok, st, NEG) if masked else st

        _flash_tile(MLA_H, score, lambda hh: 0.0, vt_ref[j], state)

    def body(j, carry):
        tile(j, False)
        return carry

    lax.fori_loop(0, i, body, 0)
    tile(i, True)

    o_lat = jnp.concatenate([(acc_ref[hh] / l_ref[hh]).T.astype(BF16) for hh in range(MLA_H)], axis=1)
    o_ref[0] = _dot(o_lat, wuv_ref[...]).astype(BF16)


def _mla_prefill(qmla, kvmla, ckvt, wuv_bd, t):
    b, s, _ = qmla.shape
    nt = s // t
    return pl.pallas_call(
        functools.partial(_mla_prefill_kernel, t=t),
        grid=(b, nt),
        in_specs=[pl.BlockSpec((1, t, 2048), lambda bb, i: (bb, i, 0)),
                  pl.BlockSpec((1, s, 256), lambda bb, i: (bb, 0, 0)),
                  pl.BlockSpec((nt, LANE, t), lambda bb, i: (bb, 0, 0)),
                  _full((1024, 512))],
        out_specs=pl.BlockSpec((1, t, 512), lambda bb, i: (bb, i, 0)),
        out_shape=jax.ShapeDtypeStruct((b, s, 512), BF16),
        scratch_shapes=_state_scratch(t),
        compiler_params=_params(2),
        name="mla_prefill",
    )(qmla, kvmla, ckvt, wuv_bd)


def _gelu_tanh(x):
    return 0.5 * x * (1.0 + jnp.tanh(math.sqrt(2.0 / math.pi) * (x + 0.044715 * (x * x * x))))


def _compress(load_rows, n_seg, wbig_ref, pe_ref, b1_ref, w2_ref):
    seg = jnp.concatenate([load_rows(r).astype(BF16) for r in range(CMP_STRIDE)], axis=1)
    w = wbig_ref[...]
    ab = _dot(seg, w)
    cacc = _dot(pe_ref[...].astype(BF16), w)
    half = 2 * CMP_HIDDEN
    const = b1_ref[...] + cacc[0:1, :half] + cacc[1:2, half:]
    pre = const + ab[:, :half] + pltpu.roll(ab[:, half:], n_seg - 1, 0)
    return _dot(_gelu_tanh(pre).astype(BF16), w2_ref[...])


def _compress_prefill_kernel(x_ref, wbig_ref, pe_ref, b1_ref, w2_ref, o_ref, ot_ref, *, n_seg):
    def load_rows(r):
        return x_ref[0, pl.ds(r, n_seg, stride=CMP_STRIDE), :]

    kv = _compress(load_rows, n_seg, wbig_ref, pe_ref, b1_ref, w2_ref)
    o_ref[0] = kv.astype(BF16)
    ot_ref[0] = kv.T.astype(BF16)


def _compress_prefill(nkv, lw):
    b, s, _ = nkv.shape
    n_seg = s // CMP_STRIDE
    return pl.pallas_call(
        functools.partial(_compress_prefill_kernel, n_seg=n_seg),
        grid=(b,),
        in_specs=[pl.BlockSpec((1, s, LANE), lambda bb: (bb, 0, 0)),
                  _full((CMP_STRIDE * LANE, 512)), _full((16, CMP_STRIDE * LANE)),
                  _full((1, 256)), _full((256, LANE))],
        out_specs=[pl.BlockSpec((1, n_seg, LANE), lambda bb: (bb, 0, 0)),
                   pl.BlockSpec((1, LANE, n_seg), lambda bb: (bb, 0, 0))],
        out_shape=[jax.ShapeDtypeStruct((b, n_seg, LANE), BF16),
                   jax.ShapeDtypeStruct((b, LANE, n_seg), BF16)],
        compiler_params=_params(1),
        name="nsa_compress",
    )(nkv, lw["cmp_wbig"], lw["cmp_pe"], lw["cmp_b1"], lw["cmp_w2"])


def _select_blocks(imp, qpos, ns, n_sel, axis):
    shape = imp.shape
    nsp = shape[axis]
    blk = _iota(shape, axis)
    cur = lax.shift_right_logical(qpos, SEL_SHIFT)
    ok_s = blk * SEL_BLOCK <= qpos
    forced = (blk == 0) | (blk == cur) | (blk == cur - 1)
    score = jnp.where(ok_s, jnp.where(forced, FORCE, imp), -FORCE)
    score = jnp.where(blk < ns, score, -jnp.inf)
    blk_f = blk.astype(F32)
    sel = jnp.zeros(shape, F32)
    idxs = []
    for _ in range(n_sel):
        mx = jnp.max(score, axis=axis, keepdims=True)
        idx = jnp.min(jnp.where(score == mx, blk_f, float(nsp)), axis=axis, keepdims=True)
        hit = blk_f == idx
        sel = jnp.where(hit, 1.0, sel)
        score = jnp.where(hit, -jnp.inf, score)
        idxs.append(idx)
    return sel, idxs


def _nsa_prefill_kernel(q_ref, cmp_ref, cmpt_ref, slc_ref, slct_ref, win_ref, wint_ref, gt_ref, c2st_ref,
                        o_ref, *scratch, t, ns, n_sel):
    state, oacc_ref = scratch[:6], scratch[6]
    m_ref, l_ref, acc_ref = state[:3]
    i = pl.program_id(1)
    q0 = i * t
    qpos = q0 + _iota((1, t), 1)
    q_slabs = [q_ref[0, :, hh * LANE:(hh + 1) * LANE] for hh in range(NSA_H)]
    slopes = [_alibi(NSA_H, hh) for hh in range(NSA_H)]

    def gate(hh, g):
        c = hh * 3 + g
        return gt_ref[0, c:c + 1, :]

    kvcmp = cmp_ref[0]
    kvcmpt = cmpt_ref[0]
    ncp = kvcmp.shape[0]
    cmp_end = _iota((ncp, t), 0) * CMP_STRIDE + (CMP_BLOCK - 1)
    ok_c = (qpos - cmp_end) >= 0
    rel_c = (cmp_end - q0).astype(F32)
    p_sum = jnp.zeros((ncp, t), F32)
    for hh in range(NSA_H):
        st = jnp.where(ok_c, _dot_nt(kvcmp, q_slabs[hh]) + slopes[hh] * rel_c, NEG)
        e = jnp.exp(st - jnp.max(st, axis=0, keepdims=True))
        p = jnp.where(ok_c, e / jnp.sum(e, axis=0, keepdims=True), 0.0)
        oacc_ref[hh] = gate(hh, 0) * _dot(kvcmpt, p.astype(BF16))
        p_sum = p_sum + p
    p_hi = p_sum.astype(BF16)
    p_lo = (p_sum - p_hi.astype(F32)).astype(BF16)
    c2st = c2st_ref[...]
    imp = _dot(c2st, p_hi) + _dot(c2st, p_lo)
    sel, _ = _select_blocks(imp, qpos, ns, n_sel, 0)
    sel_b = sel.astype(BF16)
    nsp = sel.shape[0]

    key_in_tile = _iota((t, t), 0).astype(F32)
    biases = [slopes[hh] * key_in_tile for hh in range(NSA_H)]

    _init_state(m_ref, l_ref, acc_ref)

    def slc_tile(j, carry):
        k0 = pl.multiple_of(j * t, t)
        kv = slc_ref[0, pl.ds(k0, t), :]
        kvt = slct_ref[j]
        kblk = lax.shift_right_logical(k0 + _iota((t, nsp), 0), SEL_SHIFT)
        expand = jnp.where(kblk == _iota((t, nsp), 1), 1.0, 0.0).astype(BF16)
        member = _dot(expand, sel_b)
        ok = jnp.where(_rel_dist_t(q0, k0, t) >= 0, member, 0.0) > 0.5
        tile_off = (k0 - q0).astype(F32)
        _flash_tile(NSA_H, lambda hh: jnp.where(ok, _dot_nt(kv, q_slabs[hh]) + biases[hh], NEG),
                    lambda hh: slopes[hh] * tile_off, kvt, state)
        return carry

    lax.fori_loop(0, i + 1, slc_tile, 0)
    for hh in range(NSA_H):
        oacc_ref[hh] = oacc_ref[hh] + gate(hh, 1) * (acc_ref[hh] / l_ref[hh])

    _init_state(m_ref, l_ref, acc_ref)
    n_band = -(-WINDOW // t) + 1
    for back in range(n_band):
        j_raw = i - back
        j = jnp.maximum(j_raw, 0)
        k0 = pl.multiple_of(j * t, t)
        kv = win_ref[0, pl.ds(k0, t), :]
        kvt = wint_ref[j]
        dist = _rel_dist_t(q0, k0, t)
        dist_chk = dist + jnp.where(j_raw >= 0, 0, 2 * WINDOW + t)
        ok = jnp.where(dist >= 0, dist_chk, 2 * WINDOW) <= WINDOW
        tile_off = (k0 - q0).astype(F32)
        _flash_tile(NSA_H, lambda hh: jnp.where(ok, _dot_nt(kv, q_slabs[hh]) + biases[hh], NEG),
                    lambda hh: slopes[hh] * tile_off, kvt, state)
    for hh in range(NSA_H):
        o = oacc_ref[hh] + gate(hh, 2) * (acc_ref[hh] / l_ref[hh])
        o_ref[0, :, hh * LANE:(hh + 1) * LANE] = o.T.astype(BF16)


def _nsa_prefill(nq, kvcmp, kvcmpt, slcb, slct, nwinb, wint, gatet, c2st, t):
    b, s, _ = nq.shape
    nt = s // t
    ns = s // SEL_BLOCK
    ncp = kvcmp.shape[1]
    nsp = c2st.shape[0]
    kern = functools.partial(_nsa_prefill_kernel, t=t, ns=ns, n_sel=min(N_SEL, ns))
    resident = pl.BlockSpec((1, s, LANE), lambda bb, i: (bb, 0, 0))
    resident_t = pl.BlockSpec((nt, LANE, t), lambda bb, i: (bb, 0, 0))
    return pl.pallas_call(
        kern,
        grid=(b, nt),
        in_specs=[pl.BlockSpec((1, t, 1024), lambda bb, i: (bb, i, 0)),
                  pl.BlockSpec((1, ncp, LANE), lambda bb, i: (bb, 0, 0)),
                  pl.BlockSpec((1, LANE, ncp), lambda bb, i: (bb, 0, 0)),
                  resident, resident_t, resident, resident_t,
                  pl.BlockSpec((1, LANE, t), lambda bb, i: (bb * nt + i, 0, 0)),
                  _full((nsp, ncp))],
        out_specs=pl.BlockSpec((1, t, 1024), lambda bb, i: (bb, i, 0)),
        out_shape=jax.ShapeDtypeStruct((b, s, 1024), BF16),
        scratch_shapes=_state_scratch(t) + [pltpu.VMEM((8, LANE, t), F32)],
        compiler_params=_params(2),
        name="nsa_prefill",
    )(nq, kvcmp, kvcmpt, slcb, slct, nwinb, wint, gatet, c2st)


def _merge_kernel(x_ref, od_ref, om_ref, on_ref, brg_ref, wb0_ref, wb1_ref, wb2_ref, wo_ref,
                  gx_ref, wq_ref, x1_ref, hq_ref):
    merged = brg_ref[:, 0:1024] * _dot(od_ref[...].astype(BF16), wb0_ref[...])
    merged = merged + brg_ref[:, 1024:2048] * _dot(om_ref[...].astype(BF16), wb1_ref[...])
    merged = merged + brg_ref[:, 2048:3072] * _dot(on_ref[...].astype(BF16), wb2_ref[...])
    x1 = x_ref[...] + _dot(merged.astype(BF16), wo_ref[...])
    x1_ref[...] = x1
    hq_ref[...] = _dot(_rms(x1, gx_ref[...]).astype(BF16), wq_ref[...]).astype(BF16)


def _merge(x, o_diff, o_mla, o_nsa, brg, lw):
    m = x.shape[0]
    tm = min(256, m)

    def row(width):
        return pl.BlockSpec((tm, width), lambda i: (i, 0))

    return pl.pallas_call(
        _merge_kernel,
        grid=(m // tm,),
        in_specs=[row(1024), row(512), row(512), row(1024), row(3072),
                  _full((512, 1024)), _full((512, 1024)), _full((1024, 1024)), _full((1024, 1024)),
                  _full((1, 1024)), _full((1024, 1024))],
        out_specs=[row(1024), row(1024)],
        out_shape=[jax.ShapeDtypeStruct((m, 1024), F32), jax.ShapeDtypeStruct((m, 1024), BF16)],
        compiler_params=_params(1),
        name="merge_out",
    )(x, o_diff, o_mla, o_nsa, brg, lw["wb0"], lw["wb1"], lw["wb2"], lw["w_out"],
      lw["ln_xattn"], lw["mem_wq_bd"])


def _xattn_rows(hq, mem):
    mk = mem[:, :256]
    mv = mem[:, 256:]
    lane_head = lax.shift_right_logical(_iota((1, 256), 1), int(math.log2(MEM_HD)))
    out = jnp.zeros((hq.shape[0], 256), F32)
    for hh in range(MEM_H):
        s = _dot_nt(hq[:, hh * 256:(hh + 1) * 256], mk)
        e = jnp.exp(s - jnp.max(s, axis=1, keepdims=True))
        p = e / jnp.sum(e, axis=1, keepdims=True)
        out = out + jnp.where(lane_head == hh, _dot(p.astype(BF16), mv), 0.0)
    return out


def _xattn_prefill_kernel(hq_ref, mem_ref, o_ref):
    o_ref[0] = _xattn_rows(hq_ref[0], mem_ref[0].astype(BF16)).astype(BF16)


def _xattn_prefill(hq, mem_kv, t):
    b, s, _ = hq.shape
    n_mem = mem_kv.shape[1]
    return pl.pallas_call(
        _xattn_prefill_kernel,
        grid=(b, s // t),
        in_specs=[pl.BlockSpec((1, t, 1024), lambda bb, i: (bb, i, 0)),
                  pl.BlockSpec((1, n_mem, 512), lambda bb, i: (bb, 0, 0))],
        out_specs=pl.BlockSpec((1, t, 256), lambda bb, i: (bb, i, 0)),
        out_shape=jax.ShapeDtypeStruct((b, s, 256), BF16),
        compiler_params=_params(2),
        name="xattn_prefill",
    )(hq, mem_kv)


def _xattn_one(hq_row, mem):
    q = jnp.concatenate([hq_row[:, hh * 256:(hh + 1) * 256] for hh in range(MEM_H)]
                        + [jnp.zeros((8 - MEM_H, 256), F32)], axis=0).astype(BF16)
    s = _dot_nt(q, mem[:, :256])
    e = jnp.exp(s - jnp.max(s, axis=1, keepdims=True))
    p = e / jnp.sum(e, axis=1, keepdims=True)
    pv = _dot(p.astype(BF16), mem[:, 256:])
    lane_head = lax.shift_right_logical(_iota((8, 256), 1), int(math.log2(MEM_HD)))
    return jnp.sum(jnp.where(lane_head == _iota((8, 256), 0), pv, 0.0), axis=0, keepdims=True)


def _xattn_decode_kernel(hq_ref, mem_ref, o_ref, *, group):
    hq_all = hq_ref[...].astype(F32)
    for g in range(group):
        o_ref[g:g + 1, :] = _xattn_one(hq_all[g:g + 1, :], mem_ref[g, 0].astype(BF16))


def _xattn_decode(hq, cache_mem, layer):
    db = hq.shape[0]
    n_mem = cache_mem.shape[2]
    group = 8
    return pl.pallas_call(
        functools.partial(_xattn_decode_kernel, group=group),
        grid=(db // group,),
        in_specs=[pl.BlockSpec((group, 1024), lambda i: (i, 0)),
                  pl.BlockSpec((group, 1, n_mem, 512), lambda i: (i, layer, 0, 0))],
        out_specs=pl.BlockSpec((group, 256), lambda i: (i, 0)),
        out_shape=jax.ShapeDtypeStruct((db, 256), F32),
        compiler_params=_params(1),
        name="xattn_decode",
    )(hq, cache_mem)


def _ffn_kernel(x_ref, a_ref, wo_ref, g_ref, wgu_ref, wd_ref, gf_ref, o_ref, *, final, chunk):
    x2 = x_ref[...] + _dot(a_ref[...].astype(BF16), wo_ref[...])
    hn = _rms(x2, g_ref[...]).astype(BF16)
    acc = x2
    for c in range(D_FF // chunk):
        g = _dot(hn, wgu_ref[:, c * chunk:(c + 1) * chunk])
        u = _dot(hn, wgu_ref[:, D_FF + c * chunk:D_FF + (c + 1) * chunk])
        act = (g * _sigmoid(g) * u).astype(BF16)
        acc = acc + _dot(act, wd_ref[c * chunk:(c + 1) * chunk, :])
    if final:
        acc = _rms(acc, gf_ref[...])
    o_ref[...] = acc


def _ffn(x1, attn, lw, ln_final, final):
    m = x1.shape[0]
    tm = min(256, m)
    chunk = 1408
    assert D_FF % chunk == 0

    def row(width):
        return pl.BlockSpec((tm, width), lambda i: (i, 0))

    return pl.pallas_call(
        functools.partial(_ffn_kernel, final=final, chunk=chunk),
        grid=(m // tm,),
        in_specs=[row(1024), row(256), _full((256, 1024)), _full((1, 1024)),
                  _full((1024, 2 * D_FF)), _full((D_FF, 1024)), _full((1, 1024))],
        out_specs=row(1024),
        out_shape=jax.ShapeDtypeStruct((m, 1024), F32),
        compiler_params=_params(1),
        name="ffn",
    )(x1, attn, lw["mem_wo"], lw["ln_ffn"], lw["w_gu"], lw["w_down"], ln_final)


def _memkv_kernel(x_ref, g_ref, w_ref, o_ref):
    o_ref[...] = _dot(_rms(x_ref[...], g_ref[...]).astype(BF16), w_ref[...])


def _memkv(mem, g, w):
    m = mem.shape[0]
    tm = min(256, m)
    return pl.pallas_call(
        _memkv_kernel,
        grid=(m // tm,),
        in_specs=[pl.BlockSpec((tm, 1024), lambda i: (i, 0)), _full((1, 1024)), _full((1024, 512))],
        out_specs=pl.BlockSpec((tm, 512), lambda i: (i, 0)),
        out_shape=jax.ShapeDtypeStruct((m, 512), F32),
        compiler_params=_params(1),
        name="mem_kv",
    )(mem, g, w)


def _page_specs(n_pages, block, layer, col_block=0):
    def spec(p):
        return pl.BlockSpec((1, 1) + block, lambda b, pt: (pt[b * n_pages + p], layer, 0, col_block))
    return [spec(p) for p in range(n_pages)]


def _softmax_with_new(s, s_new):
    m = jnp.maximum(jnp.max(s, axis=1, keepdims=True), s_new)
    p = jnp.exp(s - m)
    p_new = jnp.exp(s_new - m)
    denom = jnp.sum(p, axis=1, keepdims=True) + p_new
    return p, p_new, denom


def _diff_decode_kernel(pt_ref, q_ref, new_ref, slope_ref, lp_ref, sg_ref, *rest,
                        n_pages, lam_init):
    pages = rest[:n_pages]
    o_ref = rest[n_pages]
    b = pl.program_id(0)
    past = n_pages * PAGE_SIZE
    q = q_ref[0]
    kvs = [jnp.concatenate([pages[n][0, 0].astype(BF16), pages[n + 1][0, 0].astype(BF16)], axis=0)
           for n in range(0, n_pages, 2)]
    rows = 2 * PAGE_SIZE
    s = jnp.concatenate([_dot_nt(q, kv[:, :LANE]) for kv in kvs], axis=1)
    dist = (past - _iota((1, past), 1)).astype(F32)
    s = s - slope_ref[...] * dist
    new = new_ref[pl.ds(b % 8, 1), :]
    s_new = jnp.sum(q.astype(F32) * new[:, :LANE], axis=1, keepdims=True)
    p, p_new, denom = _softmax_with_new(s, s_new)
    pb = p.astype(BF16)
    acc = p_new * new[:, LANE:]
    for n, kv in enumerate(kvs):
        acc = acc + _dot(pb[:, n * rows:(n + 1) * rows], kv[:, LANE:])
    o = acc / denom
    lam = _diff_lambda(lp_ref, lam_init)
    for hh in range(DIFF_H):
        oh = o[2 * hh:2 * hh + 1] - lam * o[2 * hh + 1:2 * hh + 2]
        o_ref[0, hh:hh + 1, :] = _rms(oh, sg_ref[...]) * (1.0 - lam_init)


def _diff_decode(pt_flat, q3, new_rows, slopes, lp, sg, cache, layer, lam_init, n_pages):
    db = q3.shape[0]
    kern = functools.partial(_diff_decode_kernel, n_pages=n_pages, lam_init=lam_init)
    grid_spec = pltpu.PrefetchScalarGridSpec(
        num_scalar_prefetch=1,
        grid=(db,),
        in_specs=[pl.BlockSpec((1, 8, LANE), lambda b, pt: (b, 0, 0)),
                  pl.BlockSpec((8, 256), lambda b, pt: (b // 8, 0)),
                  pl.BlockSpec((8, 1), lambda b, pt: (0, 0)),
                  pl.BlockSpec((4, DIFF_HD), lambda b, pt: (0, 0)),
                  pl.BlockSpec((1, 2 * DIFF_HD), lambda b, pt: (0, 0))]
        + _page_specs(n_pages, (PAGE_SIZE, 256), layer),
        out_specs=pl.BlockSpec((1, DIFF_H, LANE), lambda b, pt: (b, 0, 0)),
    )
    return pl.pallas_call(
        kern,
        grid_spec=grid_spec,
        out_shape=jax.ShapeDtypeStruct((db, DIFF_H, LANE), F32),
        compiler_params=_params(1),
        name="diff_decode",
    )(pt_flat, q3, new_rows, slopes, lp, sg, *([cache] * n_pages))


def _mla_decode_kernel(pt_ref, q_ref, new_ref, wuv_ref, *rest, n_pages):
    pages = rest[:n_pages]
    o_ref = rest[n_pages]
    kvt_ref = rest[n_pages + 1]
    b = pl.program_id(0)
    scale = (MLA_NOPE + MLA_ROPE) ** -0.5
    width = MLA_KV_LORA + MLA_ROPE
    past = n_pages * PAGE_SIZE
    kvt_ref[width:, :] = jnp.zeros((256 - width, past), BF16)
    for n, pg in enumerate(pages):
        kvt_ref[:width, n * PAGE_SIZE:(n + 1) * PAGE_SIZE] = pg[0, 0].astype(BF16)
    q = q_ref[0]
    kvt = kvt_ref[...]
    s = _dot(q, kvt) * scale
    new = new_ref[pl.ds(b % 8, 1), :]
    s_new = jnp.sum(q.astype(F32) * new, axis=1, keepdims=True) * scale
    p, p_new, denom = _softmax_with_new(s, s_new)
    acc = p_new * new[:, :LANE] + _dot_nt(p.astype(BF16), kvt[:LANE, :])
    o_lat = acc / denom
    out = jnp.zeros((1, 512), F32)
    for hh in range(MLA_H):
        out = out + _dot(jnp.broadcast_to(o_lat[hh:hh + 1], (8, LANE)).astype(BF16),
                         wuv_ref[hh * LANE:(hh + 1) * LANE, :])[0:1]
    o_ref[0] = out


def _mla_decode(pt_flat, q3, new_rows, wuv_bd, cache_t, layer, n_pages):
    db = q3.shape[0]
    width = MLA_KV_LORA + MLA_ROPE
    grid_spec = pltpu.PrefetchScalarGridSpec(
        num_scalar_prefetch=1,
        grid=(db,),
        in_specs=[pl.BlockSpec((1, 8, 256), lambda b, pt: (b, 0, 0)),
                  pl.BlockSpec((8, 256), lambda b, pt: (b // 8, 0)),
                  pl.BlockSpec((1024, 512), lambda b, pt: (0, 0))]
        + _page_specs(n_pages, (width, PAGE_SIZE), layer),
        out_specs=pl.BlockSpec((1, 1, 512), lambda b, pt: (b, 0, 0)),
        scratch_shapes=[pltpu.VMEM((256, n_pages * PAGE_SIZE), BF16)],
    )
    return pl.pallas_call(
        functools.partial(_mla_decode_kernel, n_pages=n_pages),
        grid_spec=grid_spec,
        out_shape=jax.ShapeDtypeStruct((db, 1, 512), F32),
        compiler_params=_params(1),
        name="mla_decode",
    )(pt_flat, q3, new_rows, wuv_bd, *([cache_t] * n_pages))


def _nsa_decode_cmp_kernel(pt_ref, q_ref, slope_ref, wbig_ref, pe_ref, b1_ref, w2_ref, c2s_ref, *rest,
                           n_pages):
    pages = rest[:n_pages]
    oc_ref, imp_ref = rest[n_pages], rest[n_pages + 1]
    seg_per_page = PAGE_SIZE // CMP_STRIDE
    n_seg = n_pages * seg_per_page
    past = n_pages * PAGE_SIZE

    def load_rows(r):
        return jnp.concatenate([pg[0, 0, pl.ds(r, seg_per_page, stride=CMP_STRIDE), :] for pg in pages],
                               axis=0)

    kvcmp = _compress(load_rows, n_seg, wbig_ref, pe_ref, b1_ref, w2_ref).astype(BF16)
    q = q_ref[0].astype(F32)
    qh = jnp.concatenate([q[:, hh * LANE:(hh + 1) * LANE] for hh in range(NSA_H)], axis=0).astype(BF16)
    cmp_end = _iota((1, n_seg), 1) * CMP_STRIDE + (CMP_BLOCK - 1)
    dist_i = past - cmp_end
    ok = dist_i >= 0
    s = jnp.where(ok, _dot_nt(qh, kvcmp) - slope_ref[...] * dist_i.astype(F32), NEG)
    e = jnp.exp(s - jnp.max(s, axis=1, keepdims=True))
    p = jnp.where(ok, e / jnp.sum(e, axis=1, keepdims=True), 0.0)
    o_c = _dot(p.astype(BF16), kvcmp)
    for hh in range(NSA_H):
        oc_ref[0, :, hh * LANE:(hh + 1) * LANE] = o_c[hh:hh + 1]
    p_sum = jnp.broadcast_to(jnp.sum(p, axis=0, keepdims=True), (8, n_seg))
    p_hi = p_sum.astype(BF16)
    p_lo = (p_sum - p_hi.astype(F32)).astype(BF16)
    c2s = c2s_ref[...]
    imp_ref[0] = (_dot(p_hi, c2s) + _dot(p_lo, c2s))[0:1]


def _nsa_decode_cmp(pt_flat, nq3, slopes, lw, c2s, cache, layer, n_pages):
    db = nq3.shape[0]
    ncp, nsp = c2s.shape
    grid_spec = pltpu.PrefetchScalarGridSpec(
        num_scalar_prefetch=1,
        grid=(db,),
        in_specs=[pl.BlockSpec((1, 1, 1024), lambda b, pt: (b, 0, 0)),
                  pl.BlockSpec((8, 1), lambda b, pt: (0, 0)),
                  pl.BlockSpec((CMP_STRIDE * LANE, 512), lambda b, pt: (0, 0)),
                  pl.BlockSpec((16, CMP_STRIDE * LANE), lambda b, pt: (0, 0)),
                  pl.BlockSpec((1, 256), lambda b, pt: (0, 0)),
                  pl.BlockSpec((256, LANE), lambda b, pt: (0, 0)),
                  pl.BlockSpec((ncp, nsp), lambda b, pt: (0, 0))]
        + _page_specs(n_pages, (PAGE_SIZE, LANE), layer),
        out_specs=[pl.BlockSpec((1, 1, 1024), lambda b, pt: (b, 0, 0)),
                   pl.BlockSpec((1, 1, nsp), lambda b, pt: (b, 0, 0))],
    )
    return pl.pallas_call(
        functools.partial(_nsa_decode_cmp_kernel, n_pages=n_pages),
        grid_spec=grid_spec,
        out_shape=[jax.ShapeDtypeStruct((db, 1, 1024), F32),
                   jax.ShapeDtypeStruct((db, 1, nsp), F32)],
        compiler_params=_params(1),
        name="nsa_decode_cmp",
    )(pt_flat, nq3, slopes, lw["cmp_wbig"], lw["cmp_pe"], lw["cmp_b1"], lw["cmp_w2"], c2s,
      *([cache] * n_pages))


def _select_decode_kernel(imp_ref, o_ref, *, past, ns, n_sel):
    imp = imp_ref[...]
    db = imp.shape[0]
    _, idxs = _select_blocks(imp, jnp.full((db, 1), past, jnp.int32), ns, n_sel, 1)
    lane = _iota((db, LANE), 1)
    out = jnp.zeros((db, LANE), F32)
    for k, idx in enumerate(idxs):
        out = jnp.where(lane == k, idx, out)
    o_ref[...] = out.astype(jnp.int32)


def _select_decode(imp, past, ns, n_sel):
    db, nsp = imp.shape
    return pl.pallas_call(
        functools.partial(_select_decode_kernel, past=past, ns=ns, n_sel=n_sel),
        grid=(1,),
        in_specs=[_full((db, nsp))],
        out_specs=_full((db, LANE)),
        out_shape=jax.ShapeDtypeStruct((db, LANE), jnp.int32),
        compiler_params=_params(1),
        name="nsa_decode_select",
    )(imp)


def _nsa_decode_sel_kernel(pg_ref, hf_ref, ix_ref, q_ref, newkv_ref, neww_ref, gate_ref, slope_ref,
                           oc_ref, win_ref, *rest, n_sel, past):
    blocks = rest[:n_sel]
    o_ref, nwin_ref = rest[n_sel], rest[n_sel + 1]
    kv_ref = rest[n_sel + 2]
    b = pl.program_id(0)
    row = pl.ds(b % 8, 1)
    q = q_ref[0].astype(F32)
    qf = jnp.concatenate([q[:, hh * LANE:(hh + 1) * LANE] for hh in range(NSA_H)], axis=0)
    qs = qf.astype(BF16)
    slope = slope_ref[...]
    gates = gate_ref[row, :]

    cur_blk = past // SEL_BLOCK
    pos_rows = []
    n_new = jnp.int32(0)
    for k in range(n_sel):
        kv_ref[k * SEL_BLOCK:(k + 1) * SEL_BLOCK, :] = blocks[k][0, 0].astype(BF16)
        idx = ix_ref[b * n_sel + k]
        idx_pos = jnp.where(idx < cur_blk, idx, cur_blk + 1)
        pos_rows.append(idx_pos * SEL_BLOCK + _iota((1, SEL_BLOCK), 1))
        n_new = n_new + (idx == cur_blk).astype(jnp.int32)
    kv = kv_ref[...]
    dist_i = past - jnp.concatenate(pos_rows, axis=1)
    ok = dist_i >= 0
    s = jnp.where(ok, _dot_nt(qs, kv) - slope * dist_i.astype(F32), NEG)
    newkv = newkv_ref[row, :]
    s_new = jnp.sum(qf * newkv, axis=1, keepdims=True) + jnp.where(n_new > 0, 0.0, NEG)
    p, p_new, denom = _softmax_with_new(s, s_new)
    o_s = (p_new * newkv + _dot(p.astype(BF16), kv)) / denom

    win = win_ref[0, 0]
    n_win = win.shape[0]
    winb = win.astype(BF16)
    dist_w = (n_win - _iota((1, n_win), 1)).astype(F32)
    ok_w = dist_w <= float(WINDOW)
    s_w = jnp.where(ok_w, _dot_nt(qs, winb) - slope * dist_w, NEG)
    neww = neww_ref[row, :]
    s_wn = jnp.sum(qf * neww, axis=1, keepdims=True)
    pw, pw_new, denom_w = _softmax_with_new(s_w, s_wn)
    o_w = (pw_new * neww + _dot(pw.astype(BF16), winb)) / denom_w

    lane = _iota((8, LANE), 1)
    head3 = _iota((8, LANE), 0) * 3

    def gate(g):
        return jnp.sum(jnp.where(lane == head3 + g, gates, 0.0), axis=1, keepdims=True)

    o_c = jnp.concatenate([oc_ref[0, :, hh * LANE:(hh + 1) * LANE] for hh in range(NSA_H)], axis=0)
    o = gate(0) * o_c + gate(1) * o_s + gate(2) * o_w
    for hh in range(NSA_H):
        o_ref[0, :, hh * LANE:(hh + 1) * LANE] = o[hh:hh + 1]

    rolled = pltpu.roll(win, n_win - 1, 0)
    nwin_ref[0, 0] = jnp.where(_iota((n_win, LANE), 0) == n_win - 1, neww, rolled)


def _nsa_decode_sel(sel_pg, sel_hf, sel_ix, nq3, slc_new, win_new, gates, slopes, o_c, state_win, cache,
                    layer, past):
    db = nq3.shape[0]
    n_sel = sel_ix.shape[0] // db
    n_win = state_win.shape[2]

    def blk_spec(k):
        return pl.BlockSpec((1, 1, SEL_BLOCK, LANE),
                            lambda b, pg, hf, ix: (pg[b * n_sel + k], layer, hf[b * n_sel + k], 1))

    def c3(b, pg, hf, ix):
        return (b, 0, 0)

    def g8(b, pg, hf, ix):
        return (b // 8, 0)

    grid_spec = pltpu.PrefetchScalarGridSpec(
        num_scalar_prefetch=3,
        grid=(db,),
        in_specs=[pl.BlockSpec((1, 1, 1024), c3),
                  pl.BlockSpec((8, LANE), g8), pl.BlockSpec((8, LANE), g8), pl.BlockSpec((8, LANE), g8),
                  pl.BlockSpec((8, 1), lambda b, pg, hf, ix: (0, 0)),
                  pl.BlockSpec((1, 1, 1024), c3),
                  pl.BlockSpec((1, 1, n_win, LANE), lambda b, pg, hf, ix: (b, layer, 0, 0))]
        + [blk_spec(k) for k in range(n_sel)],
        out_specs=[pl.BlockSpec((1, 1, 1024), c3),
                   pl.BlockSpec((1, 1, n_win, LANE), lambda b, pg, hf, ix: (b, 0, 0, 0))],
        scratch_shapes=[pltpu.VMEM((n_sel * SEL_BLOCK, LANE), BF16)],
    )
    return pl.pallas_call(
        functools.partial(_nsa_decode_sel_kernel, n_sel=n_sel, past=past),
        grid_spec=grid_spec,
        out_shape=[jax.ShapeDtypeStruct((db, 1, 1024), F32),
                   jax.ShapeDtypeStruct((db, 1, n_win, LANE), F32)],
        compiler_params=_params(1),
        name="nsa_decode_sel",
    )(sel_pg, sel_hf, sel_ix, nq3, slc_new, win_new, gates, slopes, o_c, state_win, *([cache] * n_sel))


def _pad_cols(w, width):
    return jnp.pad(w, ((0, 0), (0, width - w.shape[1])))


def _block_diag(blocks):
    n, r, c = blocks.shape
    eye = jnp.eye(n, dtype=blocks.dtype)
    return (blocks[:, :, None, :] * eye[:, None, :, None]).reshape(n * r, n * c)


def _layer_weights(l, p):
    w = p["w_in"][l]
    cuts = np.cumsum([0, 512, 256, 256, 128, 32, 512, 256, 128, 24, 3072])
    d_q, d_kv, m_cq, m_ckv, m_kr, n_q, n_kv, n_win, n_g, br_g = [w[:, cuts[k]:cuts[k + 1]] for k in range(10)]
    eye2 = jnp.eye(2, dtype=F32)
    d_q = d_q * (DIFF_HD ** -0.5)
    n_q = n_q * (NSA_HD ** -0.5)
    dq_bd = (d_q.reshape(D_MODEL, DIFF_H, 2, 1, DIFF_HD) * eye2[None, None, :, :, None]).reshape(D_MODEL, 1024)
    half = MLA_ROPE // 2
    m_kr_sw = jnp.concatenate([m_kr[:, half:], m_kr[:, :half]], axis=1)
    nq_pad = jnp.pad(n_q.reshape(D_MODEL, NSA_H, NSA_HD), ((0, 0), (0, 0), (0, LANE - NSA_HD))).reshape(D_MODEL, 1024)
    w_all = jnp.concatenate([dq_bd, d_kv, m_cq, m_ckv, _pad_cols(m_kr, LANE), _pad_cols(m_kr_sw, LANE),
                             nq_pad, n_kv, n_win, _pad_cols(n_g, LANE), br_g], axis=1).astype(BF16)
    assert w_all.shape[1] == N_PROJ

    wuq = p["mla_w_uq"][l].reshape(MLA_Q_LORA, MLA_H, MLA_NOPE + MLA_ROPE)
    wn = wuq[:, :, :MLA_NOPE].reshape(MLA_Q_LORA, MLA_H * MLA_NOPE)
    wr = wuq[:, :, MLA_NOPE:]
    wr_sw = jnp.concatenate([wr[:, :, half:], wr[:, :, :half]], axis=2)
    pad3 = ((0, 0), (0, 0), (0, LANE - MLA_ROPE))
    wr_p = jnp.pad(wr, pad3).reshape(MLA_Q_LORA, MLA_H * LANE)
    wrs_p = jnp.pad(wr_sw, pad3).reshape(MLA_Q_LORA, MLA_H * LANE)
    wuk_bd = _block_diag(jnp.transpose(p["mla_w_uk"][l], (1, 2, 0)))
    wuv_bd = _block_diag(jnp.transpose(p["mla_w_uv"][l], (1, 0, 2)))

    w1 = p["nsa_cmp_w1"][l].reshape(2, 2, CMP_STRIDE, NSA_HD, CMP_HIDDEN)
    zero = jnp.zeros((CMP_STRIDE, NSA_HD, CMP_HIDDEN), F32)
    top = jnp.concatenate([w1[0, 0], zero, w1[0, 1], zero], axis=2)
    bot = jnp.concatenate([zero, w1[1, 0], zero, w1[1, 1]], axis=2)
    cmp_wbig = jnp.concatenate([top, bot], axis=1).reshape(CMP_STRIDE * LANE, 512).astype(BF16)
    pe = p["nsa_cmp_pe"][l]
    pe_rows = jnp.concatenate([pe[0], pe[1]], axis=1)
    cmp_pe = jnp.zeros((16, CMP_STRIDE * LANE), F32)
    cmp_pe = cmp_pe.at[0].set(pe_rows[:CMP_STRIDE].reshape(-1)).at[1].set(pe_rows[CMP_STRIDE:].reshape(-1))
    cmp_b1 = p["nsa_cmp_b1"][l].reshape(1, 2 * CMP_HIDDEN)
    cmp_w2 = _block_diag(p["nsa_cmp_w2"][l]).astype(BF16)

    wb = p["w_branch"][l]
    wb2 = jnp.pad(wb[2].reshape(NSA_H, NSA_HD, D_MODEL), ((0, 0), (LANE - NSA_HD, 0), (0, 0))).reshape(1024, D_MODEL)
    wq4 = p["mem_wq"][l].reshape(D_MODEL, MEM_H, MEM_HD) * (MEM_HD ** -0.5)
    eye4 = jnp.eye(MEM_H, dtype=F32)
    mem_wq_bd = (wq4[:, :, None, :] * eye4[None, :, :, None]).reshape(D_MODEL, 1024)
    return dict(
        ln_mix=p["ln_mix"][l][None], w_all=w_all, gq=p["mla_q_norm"][l][None], gkv=p["mla_kv_norm"][l][None],
        wn=wn.astype(BF16), wuk_bd=wuk_bd.astype(BF16), wr=wr_p.astype(BF16), wrs=wrs_p.astype(BF16),
        wuv_bd=wuv_bd.astype(BF16), lp=p["diff_lambda"][l], sg=p["diff_subln"][l][None],
        cmp_wbig=cmp_wbig, cmp_pe=cmp_pe, cmp_b1=cmp_b1, cmp_w2=cmp_w2,
        wb0=wb[0].astype(BF16), wb1=wb[1].astype(BF16), wb2=wb2.astype(BF16),
        w_out=p["w_out"][l].astype(BF16), ln_xattn=p["ln_xattn"][l][None], mem_wq_bd=mem_wq_bd.astype(BF16),
        mem_wo=p["mem_wo"][l].astype(BF16), ln_ffn=p["ln_ffn"][l][None],
        w_gu=p["ffn_w_gu"][l].astype(BF16), w_down=p["ffn_w_down"][l].astype(BF16),
        ln_mem=p["ln_mem"][l][None], mem_wkv=p["mem_wkv"][l].astype(BF16),
    )


def _rope_tables(pos):
    half = MLA_ROPE // 2
    inv = ROPE_THETA ** (-np.arange(half, dtype=np.float32) / half)
    ang = pos.astype(F32)[:, None] * jnp.asarray(inv, F32)
    cos, sin = jnp.cos(ang), jnp.sin(ang)
    zeros = jnp.zeros((pos.shape[0], LANE - MLA_ROPE), F32)
    return jnp.concatenate([cos, cos, zeros], axis=1), jnp.concatenate([-sin, sin, zeros], axis=1)


def _cmp_to_sel(ncp, nsp):
    c0 = np.arange(ncp)[:, None] * CMP_STRIDE
    s0 = np.arange(nsp)[None, :] * SEL_BLOCK
    ov = np.clip(np.minimum(c0 + CMP_BLOCK, s0 + SEL_BLOCK) - np.maximum(c0, s0), 0, None)
    return ov / CMP_BLOCK


def _round_up(n, m):
    return -(-n // m) * m


def kernel(x_prompt, x_sample, mem_prompt, cache_diff, cache_mla, cache_nsa, state_nsa_win, cache_mem, page_table, ln_mix, w_in, diff_lambda, diff_subln, mla_q_norm, mla_kv_norm, mla_w_uq, mla_w_uk, mla_w_uv, nsa_cmp_pe, nsa_cmp_w1, nsa_cmp_b1, nsa_cmp_w2, w_branch, w_out, ln_xattn, ln_mem, mem_wq, mem_wkv, mem_wo, ln_ffn, ffn_w_gu, ffn_w_down, ln_final):
    p = dict(ln_mix=ln_mix, w_in=w_in, diff_lambda=diff_lambda, diff_subln=diff_subln,
             mla_q_norm=mla_q_norm, mla_kv_norm=mla_kv_norm, mla_w_uq=mla_w_uq, mla_w_uk=mla_w_uk,
             mla_w_uv=mla_w_uv, nsa_cmp_pe=nsa_cmp_pe, nsa_cmp_w1=nsa_cmp_w1, nsa_cmp_b1=nsa_cmp_b1,
             nsa_cmp_w2=nsa_cmp_w2, w_branch=w_branch, w_out=w_out, ln_xattn=ln_xattn, ln_mem=ln_mem,
             mem_wq=mem_wq, mem_wkv=mem_wkv, mem_wo=mem_wo, ln_ffn=ln_ffn, ffn_w_gu=ffn_w_gu,
             ffn_w_down=ffn_w_down)
    bsz, seq, _ = x_prompt.shape
    db = x_sample.shape[0]
    depth = ln_mix.shape[0]
    n_pages = page_table.shape[1]
    past = n_pages * PAGE_SIZE
    n_mem = mem_prompt.shape[1]
    t = min(256, seq)
    assert seq % t == 0 and seq % SEL_BLOCK == 0 and x_sample.shape[1] == 1 and db % 8 == 0
    assert n_pages % 2 == 0

    cos_p, sin_p = _rope_tables(jnp.arange(seq, dtype=jnp.int32))
    cos_s, sin_s = _rope_tables(jnp.full((min(256, db),), past, jnp.int32))
    c2st_p = jnp.asarray(_cmp_to_sel(seq // CMP_STRIDE, _round_up(seq // SEL_BLOCK, LANE)).T, BF16)
    ns_s = (past + 1 + SEL_BLOCK - 1) // SEL_BLOCK
    c2s_s = jnp.asarray(_cmp_to_sel(past // CMP_STRIDE, _round_up(ns_s, LANE)), BF16)
    pt_flat = page_table.reshape(-1)
    slopes_diff = jnp.asarray(np.repeat([_alibi(DIFF_H, h) for h in range(DIFF_H)], 2)[:, None], F32)
    slopes_nsa = jnp.asarray(np.array([_alibi(NSA_H, h) for h in range(NSA_H)])[:, None], F32)
    cache_mla_t = jnp.swapaxes(cache_mla, 2, 3)

    xp = x_prompt.reshape(bsz * seq, D_MODEL)
    xs = x_sample.reshape(db, D_MODEL)
    mem_flat = mem_prompt.reshape(bsz * n_mem, D_MODEL)
    outs = {k: [] for k in ("dp", "ds", "mp", "ms", "np", "ns", "wp", "ws", "memp")}
    for l in range(depth):
        lw = _layer_weights(l, p)
        lam_init = 0.8 - 0.6 * math.exp(-0.3 * l)
        final = l == depth - 1
        ln_f = ln_final[None]

        mem_kv = _memkv(mem_flat, lw["ln_mem"], lw["mem_wkv"]).reshape(bsz, n_mem, 512)
        (dq, dkv, dkvb, qmla, mrow, kvmla, nq, nkv, slcb, nwin, nwinb, gates, brg,
         dvt, ckvt, slct, wint, gatet) = _proj_in(xp, lw, cos_p, sin_p, seq // t, True)

        def b3(a):
            return a.reshape(bsz, seq, a.shape[-1])

        o_diff = _diff_prefill(b3(dq), b3(dkvb), dvt, lw["lp"], lw["sg"], lam_init, t)
        o_mla = _mla_prefill(b3(qmla), b3(kvmla), ckvt, lw["wuv_bd"], t)
        kvcmp, kvcmpt = _compress_prefill(b3(nkv), lw)
        o_nsa = _nsa_prefill(b3(nq), kvcmp, kvcmpt, b3(slcb), slct, b3(nwinb), wint, gatet, c2st_p, t)
        x1, hq = _merge(xp, o_diff.reshape(-1, 512), o_mla.reshape(-1, 512), o_nsa.reshape(-1, 1024), brg, lw)
        attn = _xattn_prefill(b3(hq), mem_kv, t).reshape(-1, 256)
        xp = _ffn(x1, attn, lw, ln_f, final)
        outs["dp"].append(b3(dkv)); outs["mp"].append(b3(mrow)); outs["np"].append(b3(nkv))
        outs["wp"].append(b3(nwin)[:, seq - min(WINDOW, seq):]); outs["memp"].append(mem_kv)

        (dq, dkv, dkvb, qmla, mrow, kvmla, nq, nkv, slcb, nwin, nwinb, gates, brg) = _proj_in(
            xs, lw, cos_s, sin_s, 1, False)
        o_diff = _diff_decode(pt_flat, dq.reshape(db, 8, LANE), dkv, slopes_diff, lw["lp"], lw["sg"],
                              cache_diff, l, lam_init, n_pages).reshape(db, 512)
        o_mla = _mla_decode(pt_flat, qmla.reshape(db, 8, 256), kvmla.astype(F32), lw["wuv_bd"], cache_mla_t, l,
                            n_pages).reshape(db, 512)
        nq3 = nq.reshape(db, 1, 1024)
        o_c, imp = _nsa_decode_cmp(pt_flat, nq3, slopes_nsa, lw, c2s_s, cache_nsa, l, n_pages)
        n_sel = min(N_SEL, ns_s)
        sel_ix = _select_decode(imp[:, 0], past, ns_s, n_sel)[:, :n_sel]
        cached_ix = jnp.minimum(sel_ix, past // SEL_BLOCK - 1)
        blocks_per_page = PAGE_SIZE // SEL_BLOCK
        sel_pg = jnp.take_along_axis(page_table, cached_ix // blocks_per_page, axis=1)
        sel_hf = cached_ix % blocks_per_page
        o_nsa, new_win = _nsa_decode_sel(sel_pg.reshape(-1), sel_hf.reshape(-1), sel_ix.reshape(-1), nq3,
                                         nkv[:, LANE:], nwin, gates, slopes_nsa, o_c, state_nsa_win,
                                         cache_nsa, l, past)
        x1, hq = _merge(xs, o_diff, o_mla, o_nsa.reshape(db, 1024), brg, lw)
        attn = _xattn_decode(hq, cache_mem, l)
        xs = _ffn(x1, attn, lw, ln_f, final)
        outs["ds"].append(dkv[:, None]); outs["ms"].append(mrow[:, None]); outs["ns"].append(nkv[:, None])
        outs["ws"].append(new_win[:, 0])

    y_prompt = xp.reshape(bsz, seq, D_MODEL)
    y_sample = xs.reshape(db, 1, D_MODEL)
    st = lambda k: jnp.stack(outs[k], axis=1)
    return (y_prompt, y_sample, st("dp"), st("ds"), st("mp"), st("ms"), st("np"), st("ns"),
            st("wp"), st("ws"), st("memp"))
```
